```python
import jax, jax.numpy as jnp
from jax import lax
import numpy as np

D_MODEL = 1024
BATCH = 2
SEQ = 8192
DEPTH = 1

MEM_LEN = 256
EPS = 1e-6

M_HEADS = 4
M_DV = D_MODEL // 8
M_DQK = M_DV // 2
M_CONV = 4
M_CHUNK = 64

A_HEADS = 8
A_KV_HEADS = 2
A_DH = 64
WINDOW = 128
A_BLOCK = 128

X_HEADS = 4
X_DH = D_MODEL // X_HEADS

P_HEADS = 8
P_NKEYS = 128
P_EXPERTS = P_NKEYS * P_NKEYS
P_KEY_DIM = 128
P_TOPK = 16
P_TOKEN_BLOCK = 128

IN_SIZES = (M_HEADS * M_DQK, M_HEADS * M_DQK, M_HEADS * M_DV, M_HEADS * M_DV, M_HEADS, M_HEADS,
            A_HEADS * A_DH, A_KV_HEADS * A_DH, A_KV_HEADS * A_DH)
P_IN = sum(IN_SIZES)
MIX_WIDTH = M_HEADS * M_DV + A_HEADS * A_DH

kernel_name = "hybrid_mlstm_swa_sinks_peer_layer"


def rmsnorm(x, g):
    xf = x.astype(jnp.float32)
    y = xf * lax.rsqrt(jnp.mean(xf * xf, axis=-1, keepdims=True) + EPS)
    return (y * g.astype(jnp.float32)).astype(x.dtype)


def causal_depthwise_conv(x, w, b):
    y = lax.conv_general_dilated(
        x, w[:, None, :].astype(x.dtype), window_strides=(1,),
        padding=[(w.shape[0] - 1, 0)], dimension_numbers=('NWC', 'WIO', 'NWC'),
        feature_group_count=x.shape[-1])
    return y + b.astype(x.dtype)


def mlstm_chunkwise(q, k, v, i_pre, f_pre):
    B, S, H, _ = q.shape
    L = M_CHUNK
    nc = S // L
    f32 = jnp.float32

    def chunks(t):
        t = t.astype(f32).reshape((B, nc, L, H) + t.shape[3:])
        return jnp.moveaxis(t, (1, 3), (0, 2))

    qc = chunks(q) * (M_DQK ** -0.5)
    kc = chunks(k)
    vc = chunks(v)
    ic = chunks(i_pre)
    lfc = chunks(jax.nn.log_sigmoid(f_pre.astype(f32)))
    causal = jnp.tril(jnp.ones((L, L), dtype=bool))

    def step(carry, inp):
        C, n, m = carry
        qb, kb, vb, ib, lfb = inp
        b = jnp.cumsum(lfb, axis=-1)
        log_d = jnp.where(causal, b[..., :, None] - b[..., None, :] + ib[..., None, :], -jnp.inf)
        inter = b + m[..., None]
        m_t = jnp.maximum(inter, jnp.max(log_d, axis=-1))
        d = jnp.exp(log_d - m_t[..., None])
        s = jnp.einsum('bhtk,bhsk->bhts', qb, kb) * d
        w_inter = jnp.exp(inter - m_t)
        num = jnp.einsum('bhts,bhsv->bhtv', s, vb) + w_inter[..., None] * jnp.einsum('bhvk,bhtk->bhtv', C, qb)
        den = jnp.sum(s, axis=-1) + w_inter * jnp.einsum('bhk,bhtk->bht', n, qb)
        h = num / jnp.maximum(jnp.abs(den), jnp.exp(-m_t))[..., None]
        m_new = m_t[..., -1]
        w = jnp.exp(b[..., -1:] - b + ib - m_new[..., None])
        decay = jnp.exp(b[..., -1] + m - m_new)
        C_new = decay[..., None, None] * C + jnp.einsum('bhs,bhsv,bhsk->bhvk', w, vb, kb)
        n_new = decay[..., None] * n + jnp.einsum('bhs,bhsk->bhk', w, kb)
        return (C_new, n_new, m_new), h

    init = (jnp.zeros((B, H, M_DV, M_DQK), f32), jnp.zeros((B, H, M_DQK), f32), jnp.zeros((B, H), f32))
    _, h = lax.scan(step, init, (qc, kc, vc, ic, lfc))
    h = jnp.moveaxis(h, (0, 2), (1, 3)).reshape(B, S, H, M_DV)
    return h.astype(v.dtype)


def sliding_window_gqa_sinks(q, k, v, sinks):
    B, S, Hq, dh = q.shape
    Hkv = k.shape[2]
    G = Hq // Hkv
    nb = S // A_BLOCK
    f32 = jnp.float32
    qb = q.astype(f32).reshape(B, nb, A_BLOCK, Hkv, G, dh)

    def band_blocks(t):
        t = t.astype(f32)
        prev = jnp.concatenate([jnp.zeros_like(t[:, :A_BLOCK]), t[:, :S - A_BLOCK]], axis=1)
        return jnp.concatenate([prev.reshape(B, nb, A_BLOCK, Hkv, dh),
                                t.reshape(B, nb, A_BLOCK, Hkv, dh)], axis=2)

    kb = band_blocks(k)
    vb = band_blocks(v)
    qi = jnp.arange(A_BLOCK)[:, None]
    kj = jnp.arange(2 * A_BLOCK)[None, :]
    diff = qi - kj + A_BLOCK
    band = (diff >= 0) & (diff < WINDOW)
    exists = (jnp.arange(nb)[:, None, None] > 0) | (kj >= A_BLOCK)[None]
    mask = band[None] & exists

    scores = jnp.einsum('bnqhgd,bnkhd->bnhgqk', qb, kb) * (dh ** -0.5)
    scores = jnp.where(mask[None, :, None, None], scores, -jnp.inf)
    sink = sinks.astype(f32).reshape(Hkv, G)[None, None, :, :, None, None]
    mx = jnp.maximum(jnp.max(scores, axis=-1, keepdims=True), sink)
    p = jnp.exp(scores - mx)
    probs = p / (jnp.sum(p, axis=-1, keepdims=True) + jnp.exp(sink - mx))
    out = jnp.einsum('bnhgqk,bnkhd->bnqhgd', probs, vb)
    return out.reshape(B, S, Hq * dh).astype(q.dtype)


def memory_cross_attention(xn, memn, w_q, w_k, w_v, w_o):
    B, S, D = xn.shape
    M = memn.shape[1]
    f32 = jnp.float32
    q = (xn @ w_q).reshape(B, S, X_HEADS, X_DH).astype(f32)
    k = (memn @ w_k).reshape(B, M, X_HEADS, X_DH).astype(f32)
    v = (memn @ w_v).reshape(B, M, X_HEADS, X_DH).astype(f32)
    p = jax.nn.softmax(jnp.einsum('bshd,bmhd->bhsm', q, k) * (X_DH ** -0.5), axis=-1)
    o = jnp.einsum('bhsm,bmhd->bshd', p, v).reshape(B, S, X_HEADS * X_DH).astype(xn.dtype)
    return o @ w_o


def peer_ffn(xn, w_pq, sub_keys1, sub_keys2, expert_down, expert_up):
    B, S, D = xn.shape
    T = P_TOKEN_BLOCK
    xt = xn.reshape(-1, T, D)

    def block(xb):
        qh = (xb @ w_pq).reshape(T, P_HEADS, 2, P_KEY_DIM).astype(jnp.float32)
        s1 = jnp.einsum('thd,nd->thn', qh[:, :, 0], sub_keys1.astype(jnp.float32))
        s2 = jnp.einsum('thd,nd->thn', qh[:, :, 1], sub_keys2.astype(jnp.float32))
        v1, i1 = lax.top_k(s1, P_TOPK)
        v2, i2 = lax.top_k(s2, P_TOPK)
        cand_s = (v1[..., :, None] + v2[..., None, :]).reshape(T, P_HEADS, P_TOPK * P_TOPK)
        cand_i = (i1[..., :, None] * P_NKEYS + i2[..., None, :]).reshape(T, P_HEADS, P_TOPK * P_TOPK)
        top_s, pos = lax.top_k(cand_s, P_TOPK)
        eidx = jnp.take_along_axis(cand_i, pos, axis=-1)
        g = jax.nn.softmax(top_s, axis=-1)
        u = expert_down[eidx]
        a = jax.nn.gelu(jnp.einsum('thkd,td->thk', u, xb), approximate=False)
        coef = (g * a.astype(jnp.float32)).astype(xb.dtype)
        return jnp.einsum('thk,thkd->td', coef, expert_up[eidx])

    return lax.map(block, xt).reshape(B, S, D)


def setup_inputs(seed: int = 0) -> dict:
    key = jax.random.key(seed)
    ks = jax.random.split(key, 24)
    f32 = jnp.float32

    def nrm(k, shape, scale):
        return jax.random.normal(k, shape, f32) * scale

    Dn = D_MODEL ** -0.5
    qk_ch = 2 * M_HEADS * M_DQK
    return {
        "x": nrm(ks[0], (BATCH, SEQ, D_MODEL), 1.0),
        "mem": nrm(ks[1], (BATCH, MEM_LEN, D_MODEL), 1.0),
        "g_mix": 1.0 + nrm(ks[2], (DEPTH, D_MODEL), 0.02),
        "w_in": nrm(ks[3], (DEPTH, D_MODEL, P_IN), Dn),
        "conv_w": nrm(ks[4], (DEPTH, M_CONV, qk_ch), M_CONV ** -0.5),
        "conv_b": nrm(ks[5], (DEPTH, qk_ch), 0.02),
        "b_igate": nrm(ks[6], (DEPTH, M_HEADS), 0.1),
        "b_fgate": jnp.linspace(3.0, 6.0, M_HEADS, dtype=f32)[None] + nrm(ks[7], (DEPTH, M_HEADS), 0.1),
        "g_mhead": 1.0 + nrm(ks[8], (DEPTH, M_HEADS * M_DV), 0.02),
        "sinks": nrm(ks[9], (DEPTH, A_HEADS), 0.5),
        "w_out": nrm(ks[10], (DEPTH, MIX_WIDTH, D_MODEL), MIX_WIDTH ** -0.5),
        "g_cross": 1.0 + nrm(ks[11], (DEPTH, D_MODEL), 0.02),
        "g_mem": 1.0 + nrm(ks[12], (DEPTH, D_MODEL), 0.02),
        "w_xq": nrm(ks[13], (DEPTH, D_MODEL, X_HEADS * X_DH), Dn),
        "w_xk": nrm(ks[14], (DEPTH, D_MODEL, X_HEADS * X_DH), Dn),
        "w_xv": nrm(ks[15], (DEPTH, D_MODEL, X_HEADS * X_DH), Dn),
        "w_xo": nrm(ks[16], (DEPTH, X_HEADS * X_DH, D_MODEL), (X_HEADS * X_DH) ** -0.5),
        "g_ffn": 1.0 + nrm(ks[17], (DEPTH, D_MODEL), 0.02),
        "w_pq": nrm(ks[18], (DEPTH, D_MODEL, P_HEADS * 2 * P_KEY_DIM), Dn),
        "sub_keys1": nrm(ks[19], (DEPTH, P_NKEYS, P_KEY_DIM), P_KEY_DIM ** -0.5),
        "sub_keys2": nrm(ks[20], (DEPTH, P_NKEYS, P_KEY_DIM), P_KEY_DIM ** -0.5),
        "expert_down": nrm(ks[21], (DEPTH, P_EXPERTS, D_MODEL), Dn),
        "expert_up": nrm(ks[22], (DEPTH, P_EXPERTS, D_MODEL), Dn),
        "g_final": 1.0 + nrm(ks[23], (D_MODEL,), 0.02),
    }


def reference(x, mem, g_mix, w_in, conv_w, conv_b, b_igate, b_fgate, g_mhead, sinks, w_out,
              g_cross, g_mem, w_xq, w_xk, w_xv, w_xo, g_ffn, w_pq, sub_keys1, sub_keys2,
              expert_down, expert_up, g_final):
    B, S, _ = x.shape
    split_points = [int(c) for c in np.cumsum(IN_SIZES)[:-1]]
    h = x
    for l in range(DEPTH):
        xn = rmsnorm(h, g_mix[l])
        proj = xn @ w_in[l]
        mq, mk, mv, mo, mi, mf, aq, ak, av = jnp.split(proj, split_points, axis=-1)
        qk = jax.nn.silu(causal_depthwise_conv(jnp.concatenate([mq, mk], axis=-1), conv_w[l], conv_b[l]))
        mq, mk = jnp.split(qk, 2, axis=-1)
        hm = mlstm_chunkwise(mq.reshape(B, S, M_HEADS, M_DQK), mk.reshape(B, S, M_HEADS, M_DQK),
                             mv.reshape(B, S, M_HEADS, M_DV),
                             mi + b_igate[l].astype(mi.dtype), mf + b_fgate[l].astype(mf.dtype))
        hm = rmsnorm(hm, g_mhead[l].reshape(M_HEADS, M_DV)).reshape(B, S, M_HEADS * M_DV)
        hm = (jax.nn.sigmoid(mo) * hm).astype(x.dtype)
        ha = sliding_window_gqa_sinks(aq.reshape(B, S, A_HEADS, A_DH), ak.reshape(B, S, A_KV_HEADS, A_DH),
                                      av.reshape(B, S, A_KV_HEADS, A_DH), sinks[l])
        h = h + jnp.concatenate([hm, ha], axis=-1) @ w_out[l]
        h = h + memory_cross_attention(rmsnorm(h, g_cross[l]), rmsnorm(mem, g_mem[l]),
                                       w_xq[l], w_xk[l], w_xv[l], w_xo[l])
        h = h + peer_ffn(rmsnorm(h, g_ffn[l]), w_pq[l], sub_keys1[l], sub_keys2[l],
                         expert_down[l], expert_up[l])
    return rmsnorm(h, g_final)
```

```python
import functools

import jax
import jax.numpy as jnp
from jax import lax
from jax.experimental import pallas as pl
from jax.experimental.pallas import tpu as pltpu

F32 = jnp.float32
BF16 = jnp.bfloat16

D_MODEL = 1024
BATCH = 2
SEQ = 8192
NTOK = BATCH * SEQ
MEM_LEN = 256
EPS = 1e-6

M_HEADS = 4
M_DV = 128
M_DQK = 64
M_CONV = 4
M_CHUNK = 64

A_HEADS = 8
A_KV_HEADS = 2
A_DH = 64
WINDOW = 128
A_BLOCK = 128

X_HEADS = 4
X_DH = 256

P_HEADS = 8
P_NKEYS = 128
P_EXPERTS = P_NKEYS * P_NKEYS
P_KEY_DIM = 128
P_TOPK = 16

LANES = 128
SUBLANES = 8

NEG_INF = float("-inf")

V7X_VMEM_BYTES = 64 * 1024 * 1024
MIB = 1024 * 1024


def _params(ndims, vmem_mib):
    assert vmem_mib * MIB < V7X_VMEM_BYTES
    return pltpu.CompilerParams(dimension_semantics=("arbitrary",) * ndims, vmem_limit_bytes=vmem_mib * MIB)


def _rms(xf, g):
    return xf * lax.rsqrt(jnp.mean(xf * xf, axis=-1, keepdims=True) + EPS) * g


def _dot_nt(a, b):
    return lax.dot_general(a, b, (((1,), (1,)), ((), ())), preferred_element_type=F32)


def _dot_tn(a, b):
    return lax.dot_general(a, b, (((0,), (0,)), ((), ())), preferred_element_type=F32)


def _dot(a, b):
    return jnp.dot(a, b, preferred_element_type=F32)


MIX_TM = 512
MIX_COLS = 2688


def _mix_in_kernel(x_ref, g_ref, w_ref, qk_ref, mv_ref, mo_ref, aq_ref, ak_ref, av_ref, gt_ref):
    xn = _rms(x_ref[...], g_ref[...]).astype(BF16)
    qk_ref[...] = _dot(xn, w_ref[:, 0:512])
    mv_ref[...] = _dot(xn, w_ref[:, 512:1024]).astype(BF16)
    mo_ref[...] = _dot(xn, w_ref[:, 1024:1536])
    aq_ref[...] = _dot(xn, w_ref[:, 1536:2048]).astype(BF16)
    ak_ref[...] = _dot(xn, w_ref[:, 2048:2304]).astype(BF16)
    av_ref[...] = _dot(xn, w_ref[:, 2304:2560]).astype(BF16)
    gt_ref[...] = _dot(xn, w_ref[:, 2560:2688])


def _mix_in(x2, g_mix, w_all):
    n = x2.shape[0]
    grid = (n // MIX_TM,)
    row = lambda c: pl.BlockSpec((MIX_TM, c), lambda i: (i, 0))
    full = lambda a: pl.BlockSpec(a.shape, lambda i: (0,) * a.ndim)
    return pl.pallas_call(
        _mix_in_kernel,
        grid=grid,
        in_specs=[row(D_MODEL), full(g_mix), full(w_all)],
        out_specs=[row(512), row(512), row(512), row(512), row(256), row(256), row(128)],
        out_shape=[
            jax.ShapeDtypeStruct((n, 512), F32),
            jax.ShapeDtypeStruct((n, 512), BF16),
            jax.ShapeDtypeStruct((n, 512), F32),
            jax.ShapeDtypeStruct((n, 512), BF16),
            jax.ShapeDtypeStruct((n, 256), BF16),
            jax.ShapeDtypeStruct((n, 256), BF16),
            jax.ShapeDtypeStruct((n, 128), F32),
        ],
        compiler_params=_params(1, 40),
        name="mix_in",
    )(x2, g_mix, w_all)


ML_T = 128
ML_HALO = 8


def _log_sigmoid(x):
    return jnp.minimum(x, 0.0) - jnp.log1p(jnp.exp(-jnp.abs(x)))


def _mlstm_kernel(qk_ref, halo_ref, mv_ref, mo_ref, gt_ref, cw_ref, cb_ref, gb_ref, gmh_ref,
                  hm_ref, xcat_ref, ct_ref, n_ref, m_ref):
    step = pl.program_id(0)

    @pl.when(step == 0)
    def _():
        ct_ref[...] = jnp.zeros_like(ct_ref)
        n_ref[...] = jnp.zeros_like(n_ref)
        m_ref[...] = jnp.zeros_like(m_ref)

    lane = lax.broadcasted_iota(jnp.int32, (M_CHUNK, LANES), 1)
    lo = lane < 64
    lane_row = lax.broadcasted_iota(jnp.int32, (1, LANES), 1) < 64
    r128 = lax.broadcasted_iota(jnp.int32, (LANES, LANES), 0)
    c128 = lax.broadcasted_iota(jnp.int32, (LANES, LANES), 1)
    tri_blk = ((r128 >= c128) & ((r128 < 64) == (c128 < 64))).astype(F32)
    top_rows = r128 < 64
    causal = (lax.broadcasted_iota(jnp.int32, (M_CHUNK, M_CHUNK), 0)
              >= lax.broadcasted_iota(jnp.int32, (M_CHUNK, M_CHUNK), 1))
    not_first = (step > 0).astype(F32)

    for b in range(BATCH):
        xcat_ref[b, 0:ML_HALO, :] = halo_ref[b] * not_first
        xcat_ref[b, ML_HALO:ML_HALO + ML_T, :] = qk_ref[b]
        acc = jnp.broadcast_to(cb_ref[...], (ML_T, 512))
        for j in range(M_CONV):
            acc = acc + cw_ref[j:j + 1, :] * xcat_ref[b, pl.ds(ML_HALO - (M_CONV - 1) + j, ML_T), :]
        qk = acc * jax.nn.sigmoid(acc)

        gb = gt_ref[b] + gb_ref[...]
        lf = _log_sigmoid(gb)
        bc = jnp.dot(tri_blk, lf, precision=lax.Precision.HIGHEST,
                     preferred_element_type=F32)
        b_t = bc.T
        i_t = gb.T

        for cc in range(ML_T // M_CHUNK):
            rows = slice(cc * M_CHUNK, (cc + 1) * M_CHUNK)
            for p in range(M_HEADS // 2):
                qp = qk[rows, p * 128:(p + 1) * 128] * (M_DQK ** -0.5)
                kp = qk[rows, 256 + p * 128:256 + (p + 1) * 128]
                kp_bf = kp.astype(BF16)
                sidx = b * 2 + p
                ct = ct_ref[sidx]
                ct_bf = ct.astype(BF16)
                nrow = n_ref[sidx][0:1, :]
                upd, decs, wcols = [], [], []
                for hh in range(2):
                    h = 2 * p + hh
                    mh = lo if hh == 0 else jnp.logical_not(lo)
                    qm = jnp.where(mh, qp, 0.0)
                    qm_bf = qm.astype(BF16)
                    m_prev = m_ref[b * M_HEADS + h][0:1, 0:1]
                    b_col = bc[rows, 4 + h:5 + h]
                    i_col = gb[rows, h:h + 1]
                    b_row = b_t[4 + h:5 + h, rows]
                    i_row = i_t[h:h + 1, rows]
                    log_d = jnp.where(causal, b_col - b_row + i_row, NEG_INF)
                    inter = b_col + m_prev
                    m_t = jnp.maximum(inter, jnp.max(log_d, axis=-1, keepdims=True))
                    dmat = jnp.exp(log_d - m_t)
                    s = _dot_nt(qm_bf, kp_bf) * dmat
                    w_inter = jnp.exp(inter - m_t)
                    vh = mv_ref[b, rows, h * M_DV:(h + 1) * M_DV]
                    num = _dot(s.astype(BF16), vh) + w_inter * _dot(qm_bf, ct_bf)
                    den = (jnp.sum(s, axis=-1, keepdims=True)
                           + w_inter * jnp.sum(qm * nrow, axis=-1, keepdims=True))
                    hv = num / jnp.maximum(jnp.abs(den), jnp.exp(-m_t))
                    m_new = m_t[M_CHUNK - 1:M_CHUNK, :]
                    b_last = b_col[M_CHUNK - 1:M_CHUNK, :]
                    w_col = jnp.exp(b_last - b_col + i_col - m_new)
                    decay = jnp.exp(b_last + m_prev - m_new)
                    wv = (w_col * vh.astype(F32)).astype(BF16)
                    upd.append(_dot_tn(kp_bf, wv))
                    decs.append(decay)
                    wcols.append(w_col)
                    m_ref[b * M_HEADS + h] = jnp.broadcast_to(m_new, (SUBLANES, LANES))
                    g_h = gmh_ref[:, h * M_DV:(h + 1) * M_DV]
                    gate = jax.nn.sigmoid(mo_ref[b, rows, h * M_DV:(h + 1) * M_DV])
                    hm_ref[b, rows, h * M_DV:(h + 1) * M_DV] = (gate * _rms(hv, g_h)).astype(BF16)
                dec_col = jnp.where(top_rows, decs[0], decs[1])
                ct_ref[sidx] = dec_col * ct + jnp.where(top_rows, upd[0], upd[1])
                dec_row = jnp.where(lane_row, decs[0], decs[1])
                wsel = jnp.where(lo, wcols[0], wcols[1])
                n_new = dec_row * nrow + jnp.sum(wsel * kp, axis=0, keepdims=True)
                n_ref[sidx] = jnp.broadcast_to(n_new, (SUBLANES, LANES))


def _mlstm(qk_pre, mv, mo, gates, conv_w, conv_b, gate_bias, g_mhead):
    nblk = SEQ // ML_T
    blk = lambda c: pl.BlockSpec((BATCH, ML_T, c), lambda i: (0, i, 0))
    full = lambda a: pl.BlockSpec(a.shape, lambda i: (0,) * a.ndim)
    halo = pl.BlockSpec((BATCH, ML_HALO, 512),
                        lambda i: (0, jnp.maximum(i * (ML_T // ML_HALO) - 1, 0), 0))
    return pl.pallas_call(
        _mlstm_kernel,
        grid=(nblk,),
        in_specs=[blk(512), halo, blk(512), blk(512), blk(128),
                  full(conv_w), full(conv_b), full(gate_bias), full(g_mhead)],
        out_specs=blk(512),
        out_shape=jax.ShapeDtypeStruct((BATCH, SEQ, 512), BF16),
        scratch_shapes=[
            pltpu.VMEM((BATCH, ML_HALO + ML_T, 512), F32),
            pltpu.VMEM((BATCH * 2, 128, 128), F32),
            pltpu.VMEM((BATCH * 2, SUBLANES, LANES), F32),
            pltpu.VMEM((BATCH * M_HEADS, SUBLANES, LANES), F32),
        ],
        compiler_params=_params(1, 32),
        name="mlstm",
    )(qk_pre, qk_pre, mv, mo, gates, conv_w, conv_b, gate_bias, g_mhead)


def _swa_kernel(q_ref, kp_ref, kc_ref, vp_ref, vc_ref, sk_ref, o_ref):
    n = pl.program_id(1)
    lo = lax.broadcasted_iota(jnp.int32, (A_BLOCK, LANES), 1) < 64
    qi = lax.broadcasted_iota(jnp.int32, (A_BLOCK, 2 * A_BLOCK), 0)
    kj = lax.broadcasted_iota(jnp.int32, (A_BLOCK, 2 * A_BLOCK), 1)
    diff = qi - kj + A_BLOCK
    band = (diff >= 0) & (diff < WINDOW) & ((kj >= A_BLOCK) | (n > 0))
    for j in range(A_KV_HEADS):
        cols = slice(j * 128, (j + 1) * 128)
        kk = jnp.concatenate([kp_ref[0, :, cols], kc_ref[0, :, cols]], axis=0)
        vv = jnp.concatenate([vp_ref[0, :, cols], vc_ref[0, :, cols]], axis=0)
        for p in range(2):
            c0 = j * 256 + p * 128
            qp = q_ref[0, :, c0:c0 + 128].astype(F32)
            outs = []
            for hh in range(2):
                h = j * 4 + p * 2 + hh
                mh = lo if hh == 0 else jnp.logical_not(lo)
                qm = jnp.where(mh, qp, 0.0).astype(BF16)
                sc = jnp.where(band, _dot_nt(qm, kk), NEG_INF)
                sink = sk_ref[h:h + 1, 0:1]
                mx = jnp.maximum(jnp.max(sc, axis=-1, keepdims=True), sink)
                pe = jnp.exp(sc - mx)
                den = jnp.sum(pe, axis=-1, keepdims=True) + jnp.exp(sink - mx)
                outs.append(_dot(pe.astype(BF16), vv) / den)
            o_ref[0, :, c0:c0 + 128] = jnp.where(lo, outs[0], outs[1]).astype(BF16)


def _swa(aq, ak, av, sinks_b):
    nb = SEQ // A_BLOCK
    cur = lambda c: pl.BlockSpec((1, A_BLOCK, c), lambda b, n: (b, n, 0))
    prev = lambda c: pl.BlockSpec((1, A_BLOCK, c), lambda b, n: (b, jnp.maximum(n - 1, 0), 0))
    return pl.pallas_call(
        _swa_kernel,
        grid=(BATCH, nb),
        in_specs=[cur(512), prev(256), cur(256), prev(256), cur(256),
                  pl.BlockSpec(sinks_b.shape, lambda b, n: (0, 0))],
        out_specs=cur(512),
        out_shape=jax.ShapeDtypeStruct((BATCH, SEQ, 512), BF16),
        compiler_params=_params(2, 32),
        name="swa",
    )(aq, ak, ak, av, av, sinks_b)


def _memkv_kernel(mem_ref, g_ref, wk_ref, wv_ref, k_ref, v_ref):
    mn = _rms(mem_ref[0], g_ref[...]).astype(BF16)
    k_ref[0] = _dot(mn, wk_ref[...]).astype(BF16)
    v_ref[0] = _dot(mn, wv_ref[...]).astype(BF16)


def _memkv(mem, g_mem, w_xk, w_xv):
    blk = pl.BlockSpec((1, MEM_LEN, D_MODEL), lambda b: (b, 0, 0))
    full = lambda a: pl.BlockSpec(a.shape, lambda b: (0,) * a.ndim)
    return pl.pallas_call(
        _memkv_kernel,
        grid=(BATCH,),
        in_specs=[blk, full(g_mem), full(w_xk), full(w_xv)],
        out_specs=[blk, blk],
        out_shape=[jax.ShapeDtypeStruct((BATCH, MEM_LEN, D_MODEL), BF16)] * 2,
        compiler_params=_params(1, 32),
        name="memkv",
    )(mem, g_mem, w_xk, w_xv)


POST_TM = 512


def _post_kernel(x_ref, hm_ref, ha_ref, wo_ref, gc_ref, wq_ref, km_ref, vm_ref, wxo_ref, gf_ref,
                 h2_ref, xn3_ref):
    h1 = x_ref[...] + _dot(hm_ref[...], wo_ref[0:512, :]) + _dot(ha_ref[...], wo_ref[512:1024, :])
    xn2 = _rms(h1, gc_ref[...]).astype(BF16)
    q = _dot(xn2, wq_ref[...]).astype(BF16)
    heads = []
    for hd in range(X_HEADS):
        cols = slice(hd * X_DH, (hd + 1) * X_DH)
        sc = _dot_nt(q[:, cols], km_ref[0, :, cols])
        mx = jnp.max(sc, axis=-1, keepdims=True)
        pe = jnp.exp(sc - mx)
        den = jnp.sum(pe, axis=-1, keepdims=True)
        heads.append((_dot(pe.astype(BF16), vm_ref[0, :, cols]) / den).astype(BF16))
    h2 = h1 + _dot(jnp.concatenate(heads, axis=1), wxo_ref[...])
    h2_ref[...] = h2
    xn3_ref[...] = _rms(h2, gf_ref[...]).astype(BF16)


def _post(x2, hm2, ha2, w_out, g_cross, w_xq, kmem, vmem, w_xo, g_ffn):
    n = x2.shape[0]
    per_batch = SEQ // POST_TM
    row = lambda c: pl.BlockSpec((POST_TM, c), lambda i: (i, 0))
    full = lambda a: pl.BlockSpec(a.shape, lambda i: (0,) * a.ndim)
    memblk = pl.BlockSpec((1, MEM_LEN, D_MODEL), lambda i: (i // per_batch, 0, 0))
    return pl.pallas_call(
        _post_kernel,
        grid=(n // POST_TM,),
        in_specs=[row(D_MODEL), row(512), row(512), full(w_out), full(g_cross), full(w_xq),
                  memblk, memblk, full(w_xo), full(g_ffn)],
        out_specs=[row(D_MODEL), row(D_MODEL)],
        out_shape=[jax.ShapeDtypeStruct((n, D_MODEL), F32), jax.ShapeDtypeStruct((n, D_MODEL), BF16)],
        compiler_params=_params(1, 48),
        name="post",
    )(x2, hm2, ha2, w_out, g_cross, w_xq, kmem, vmem, w_xo, g_ffn)


RT_T = 128


def _ce(vals, i, j):
    a, b = vals[i], vals[j]
    vals[i] = jnp.maximum(a, b)
    vals[j] = jnp.minimum(a, b)


def _bitonic_sort_desc(vals):
    n = len(vals)
    k = 2
    while k <= n:
        j = k // 2
        while j >= 1:
            for i in range(n):
                l = i ^ j
                if l > i:
                    if (i & k) == 0:
                        _ce(vals, i, l)
                    else:
                        _ce(vals, l, i)
            j //= 2
        k *= 2


def _bitonic_merge_desc(vals):
    n = len(vals)
    j = n // 2
    while j >= 1:
        for i in range(n):
            l = i ^ j
            if l > i:
                _ce(vals, i, l)
        j //= 2


def _merge_top16(xs, ys):
    m = [jnp.maximum(xs[a], ys[P_TOPK - 1 - a]) for a in range(P_TOPK)]
    _bitonic_merge_desc(m)
    return m


def _top16_over_keys(s_t):
    vals = [s_t[kb * SUBLANES:(kb + 1) * SUBLANES, :] for kb in range(P_NKEYS // SUBLANES)]
    _bitonic_sort_desc(vals)
    for sh in (4, 2, 1):
        rolled = [pltpu.roll(v, sh, axis=0) for v in vals]
        vals = _merge_top16(vals, rolled)
    return vals


def _route_kernel(xn_ref, wq_ref, k1_ref, k2_ref, th_ref, e1_ref, s2_ref, e2_ref):
    q_t = _dot_nt(wq_ref[...], xn_ref[...])
    sub = lax.broadcasted_iota(jnp.int32, (SUBLANES, RT_T), 0)
    s1_all, s2_all = [], []
    v1 = [None] * P_TOPK
    v2 = [None] * P_TOPK
    for h in range(P_HEADS):
        r0 = h * 2 * P_KEY_DIM
        s1 = jnp.dot(k1_ref[...], q_t[r0:r0 + P_KEY_DIM, :], precision=lax.Precision.HIGHEST,
                     preferred_element_type=F32)
        s2 = jnp.dot(k2_ref[...], q_t[r0 + P_KEY_DIM:r0 + 2 * P_KEY_DIM, :],
                     precision=lax.Precision.HIGHEST, preferred_element_type=F32)
        s1_all.append(s1)
        s2_all.append(s2)
        t1 = _top16_over_keys(s1)
        t2 = _top16_over_keys(s2)
        for a in range(P_TOPK):
            v1[a] = t1[a] if h == 0 else jnp.where(sub == h, t1[a], v1[a])
            v2[a] = t2[a] if h == 0 else jnp.where(sub == h, t2[a], v2[a])
    pairs = [(a, b) for a in range(P_TOPK) for b in range(P_TOPK // (a + 1))]
    cand_of = {ab: v1[ab[0]] + v2[ab[1]] for ab in pairs}
    pad = jnp.full((SUBLANES, RT_T), NEG_INF, F32)
    cands = [cand_of[ab] for ab in pairs] + [pad] * (64 - len(pairs))
    groups = []
    for gi in range(4):
        grp = cands[gi::4]
        _bitonic_sort_desc(grp)
        groups.append(grp)
    top = _merge_top16(_merge_top16(groups[0], groups[1]), _merge_top16(groups[2], groups[3]))
    tau = top[P_TOPK - 1]
    z = jnp.ones_like(tau)
    for kk in range(1, P_TOPK):
        z = z + jnp.exp(top[kk] - top[0])
    zinv = 1.0 / z
    inf = jnp.full((SUBLANES, RT_T), float("inf"), F32)
    theta_rank = []
    for a in range(P_TOPK):
        th_a = inf
        for b in range(P_TOPK // (a + 1)):
            th_a = jnp.minimum(th_a, jnp.where(cand_of[(a, b)] >= tau, v2[b], inf))
        theta_rank.append(th_a)
    for h in range(P_HEADS):
        s1, s2 = s1_all[h], s2_all[h]
        th_rows = jnp.full((P_NKEYS, RT_T), float("inf"), F32)
        for a in range(P_TOPK):
            th_rows = jnp.where(s1 == v1[a][h:h + 1, :], theta_rank[a][h:h + 1, :], th_rows)
        th_ref[h] = th_rows
        e1_ref[h] = jnp.exp(s1 - v1[0][h:h + 1, :]) * zinv[h:h + 1, :]
        s2_ref[h] = s2
        e2_ref[h] = jnp.exp(s2 - v2[0][h:h + 1, :])


def _route(xn3, w_pq_t, keys1, keys2):
    n = xn3.shape[0]
    full = lambda a: pl.BlockSpec(a.shape, lambda i: (0,) * a.ndim)
    oblk = pl.BlockSpec((P_HEADS, P_NKEYS, RT_T), lambda i: (0, 0, i))
    oshape = jax.ShapeDtypeStruct((P_HEADS, P_NKEYS, n), F32)
    return pl.pallas_call(
        _route_kernel,
        grid=(n // RT_T,),
        in_specs=[pl.BlockSpec((RT_T, D_MODEL), lambda i: (i, 0)), full(w_pq_t), full(keys1), full(keys2)],
        out_specs=[oblk] * 4,
        out_shape=[oshape] * 4,
        compiler_params=_params(1, 32),
        name="route",
    )(xn3, w_pq_t, keys1, keys2)


PE_T = 512
PE_E = 1024
_SQRT_HALF = 0.7071067811865476


def _peer_kernel(dn_ref, up_ref, xn_ref, th_ref, e1_ref, s2_ref, e2_ref, o_ref, a_ref, g_ref):
    e = pl.program_id(1)
    a_ref[...] = _dot_nt(dn_ref[...], xn_ref[...])
    rows_per_tile = PE_E // P_NKEYS

    assert rows_per_tile == SUBLANES
    ig0 = pl.multiple_of(e * rows_per_tile, SUBLANES)

    def body(c, carry):
        cs = pl.ds(pl.multiple_of(c * LANES, LANES), LANES)
        th8 = [th_ref[h, pl.ds(ig0, SUBLANES), cs] for h in range(P_HEADS)]
        e18 = [e1_ref[h, pl.ds(ig0, SUBLANES), cs] for h in range(P_HEADS)]
        for i in range(rows_per_tile):
            coef = jnp.zeros((P_NKEYS, LANES), F32)
            for h in range(P_HEADS):
                coef = coef + jnp.where(s2_ref[h, :, cs] >= th8[h][i:i + 1, :],
                                        e2_ref[h, :, cs] * e18[h][i:i + 1, :], 0.0)
            a = a_ref[i * P_NKEYS:(i + 1) * P_NKEYS, cs]
            gl = 0.5 * a * (1.0 + lax.erf(a * _SQRT_HALF))
            g_ref[i * P_NKEYS:(i + 1) * P_NKEYS, cs] = (gl * coef).astype(BF16)
        return carry

    lax.fori_loop(0, PE_T // LANES, body, 0)
    contrib = _dot(up_ref[...], g_ref[...])

    @pl.when(e == 0)
    def _():
        o_ref[...] = contrib

    @pl.when(e > 0)
    def _():
        o_ref[...] += contrib


def _peer(dn, up_t, xn3, th, e1, s2, e2):
    n = xn3.shape[0]
    rblk = pl.BlockSpec((P_HEADS, P_NKEYS, PE_T), lambda t, e: (0, 0, t))
    return pl.pallas_call(
        _peer_kernel,
        grid=(n // PE_T, P_EXPERTS // PE_E),
        in_specs=[pl.BlockSpec((PE_E, D_MODEL), lambda t, e: (e, 0)),
                  pl.BlockSpec((D_MODEL, PE_E), lambda t, e: (0, e)),
                  pl.BlockSpec((PE_T, D_MODEL), lambda t, e: (t, 0)),
                  rblk, rblk, rblk, rblk],
        out_specs=pl.BlockSpec((D_MODEL, PE_T), lambda t, e: (0, t)),
        out_shape=jax.ShapeDtypeStruct((D_MODEL, n), F32),
        scratch_shapes=[pltpu.VMEM((PE_E, PE_T), F32), pltpu.VMEM((PE_E, PE_T), BF16)],
        compiler_params=_params(2, 48),
        name="peer",
    )(dn, up_t, xn3, th, e1, s2, e2)


FIN_TM = 512


def _final_kernel(h_ref, pt_ref, g_ref, o_ref):
    o_ref[...] = _rms(h_ref[...] + pt_ref[...].T, g_ref[...])


def _final(h2, peer_t, g_final):
    n = h2.shape[0]
    return pl.pallas_call(
        _final_kernel,
        grid=(n // FIN_TM,),
        in_specs=[pl.BlockSpec((FIN_TM, D_MODEL), lambda i: (i, 0)),
                  pl.BlockSpec((D_MODEL, FIN_TM), lambda i: (0, i)),
                  pl.BlockSpec(g_final.shape, lambda i: (0, 0))],
        out_specs=pl.BlockSpec((FIN_TM, D_MODEL), lambda i: (i, 0)),
        out_shape=jax.ShapeDtypeStruct((n, D_MODEL), F32),
        compiler_params=_params(1, 32),
        name="final",
    )(h2, peer_t, g_final)


def _pack_w_in(w):
    o = 0
    segs = {}
    for name, sz in (("mq", 256), ("mk", 256), ("mv", 512), ("mo", 512), ("mi", 4), ("mf", 4),
                     ("aq", 512), ("ak", 128), ("av", 128)):
        segs[name] = w[:, o:o + sz]
        o += sz
    k0, k1 = segs["ak"][:, :64], segs["ak"][:, 64:]
    v0, v1 = segs["av"][:, :64], segs["av"][:, 64:]
    gates = jnp.concatenate([segs["mi"], segs["mf"], jnp.zeros((w.shape[0], LANES - 8), w.dtype)], axis=1)
    packed = jnp.concatenate([segs["mq"], segs["mk"], segs["mv"], segs["mo"],
                              segs["aq"] * (A_DH ** -0.5), k0, k0, k1, k1, v0, v0, v1, v1, gates], axis=1)
    return packed.astype(BF16)


def kernel(x, mem, g_mix, w_in, conv_w, conv_b, b_igate, b_fgate, g_mhead, sinks, w_out, g_cross, g_mem,
           w_xq, w_xk, w_xv, w_xo, g_ffn, w_pq, sub_keys1, sub_keys2, expert_down, expert_up, g_final):
    assert x.shape == (BATCH, SEQ, D_MODEL) and w_in.shape[0] == 1
    l = 0
    row = lambda v: v.reshape(1, -1).astype(F32)
    x2 = x.reshape(NTOK, D_MODEL)

    qk_pre, mv, mo, aq, ak, av, gates = _mix_in(x2, row(g_mix[l]), _pack_w_in(w_in[l]))

    gate_bias = jnp.concatenate([b_igate[l], b_fgate[l], jnp.zeros((LANES - 8,), F32)]).reshape(1, LANES)
    r3 = lambda a: a.reshape(BATCH, SEQ, a.shape[-1])
    hm = _mlstm(r3(qk_pre), r3(mv), r3(mo), r3(gates), conv_w[l], row(conv_b[l]), gate_bias, row(g_mhead[l]))

    sinks_b = jnp.broadcast_to(sinks[l].astype(F32)[:, None], (A_HEADS, LANES))
    ha = _swa(r3(aq), r3(ak), r3(av), sinks_b)

    kmem, vmem = _memkv(mem, row(g_mem[l]), w_xk[l].astype(BF16), w_xv[l].astype(BF16))

    h2, xn3 = _post(x2, hm.reshape(NTOK, 512), ha.reshape(NTOK, 512), w_out[l].astype(BF16), row(g_cross[l]),
                    (w_xq[l] * (X_DH ** -0.5)).astype(BF16), kmem, vmem, w_xo[l].astype(BF16), row(g_ffn[l]))

    th, e1, s2, e2 = _route(xn3, w_pq[l].T.astype(BF16), sub_keys1[l], sub_keys2[l])

    peer_t = _peer(expert_down[l].astype(BF16), expert_up[l].T.astype(BF16), xn3, th, e1, s2, e2)

    out = _final(h2, peer_t, row(g_final))
    return out.reshape(BATCH, SEQ, D_MODEL)
```

```python
import functools

import jax
import jax.numpy as jnp
from jax import lax
from jax.experimental import pallas as pl
from jax.experimental.pallas import tpu as pltpu

F32 = jnp.float32
BF16 = jnp.bfloat16

D_MODEL = 1024
BATCH = 2
SEQ = 8192
NTOK = BATCH * SEQ
MEM_LEN = 256
EPS = 1e-6

M_HEADS = 4
M_DV = 128
M_DQK = 64
M_CONV = 4
M_CHUNK = 64

A_HEADS = 8
A_KV_HEADS = 2
A_DH = 64
WINDOW = 128
A_BLOCK = 128

X_HEADS = 4
X_DH = 256

P_HEADS = 8
P_NKEYS = 128
P_EXPERTS = P_NKEYS * P_NKEYS
P_KEY_DIM = 128
P_TOPK = 16

LANES = 128
SUBLANES = 8
BF16_ROWS = 2 * SUBLANES

NEG_INF = float("-inf")

V7X_VMEM_BYTES = 64 * 1024 * 1024
MIB = 1024 * 1024


def _params(ndims, vmem_mib):
    assert vmem_mib * MIB < V7X_VMEM_BYTES
    return pltpu.CompilerParams(dimension_semantics=("arbitrary",) * ndims, vmem_limit_bytes=vmem_mib * MIB)


def _rms(xf, g):
    return xf * lax.rsqrt(jnp.mean(xf * xf, axis=-1, keepdims=True) + EPS) * g


def _dot_nt(a, b):
    return lax.dot_general(a, b, (((1,), (1,)), ((), ())), preferred_element_type=F32)


def _dot_tn(a, b):
    return lax.dot_general(a, b, (((0,), (0,)), ((), ())), preferred_element_type=F32)


def _dot(a, b):
    return jnp.dot(a, b, preferred_element_type=F32)


MIX_TM = 512


def _mix_in_kernel(x_ref, g_ref, w_ref, wvt_ref, qk_ref, mv_ref, mo_ref, aq_ref, ak_ref, avt_ref, gt_ref):
    xn = _rms(x_ref[...], g_ref[...]).astype(BF16)
    qk_ref[...] = _dot(xn, w_ref[:, 0:512])
    mv_ref[...] = _dot(xn, w_ref[:, 512:1024]).astype(BF16)
    mo_ref[...] = _dot(xn, w_ref[:, 1024:1536])
    aq_ref[...] = _dot(xn, w_ref[:, 1536:2048]).astype(BF16)
    ak_ref[...] = _dot(xn, w_ref[:, 2048:2304]).astype(BF16)
    gt_ref[...] = _dot(xn, w_ref[:, 2304:2432])
    avt_ref[...] = _dot_nt(wvt_ref[...], xn).astype(BF16)


def _mix_in(x2, g_mix, w_all, w_vt):
    n = x2.shape[0]
    grid = (n // MIX_TM,)
    row = lambda c: pl.BlockSpec((MIX_TM, c), lambda i: (i, 0))
    full = lambda a: pl.BlockSpec(a.shape, lambda i: (0,) * a.ndim)
    return pl.pallas_call(
        _mix_in_kernel,
        grid=grid,
        in_specs=[row(D_MODEL), full(g_mix), full(w_all), full(w_vt)],
        out_specs=[row(512), row(512), row(512), row(512), row(256),
                   pl.BlockSpec((256, MIX_TM), lambda i: (0, i)), row(128)],
        out_shape=[
            jax.ShapeDtypeStruct((n, 512), F32),
            jax.ShapeDtypeStruct((n, 512), BF16),
            jax.ShapeDtypeStruct((n, 512), F32),
            jax.ShapeDtypeStruct((n, 512), BF16),
            jax.ShapeDtypeStruct((n, 256), BF16),
            jax.ShapeDtypeStruct((256, n), BF16),
            jax.ShapeDtypeStruct((n, 128), F32),
        ],
        compiler_params=_params(1, 40),
        name="mix_in",
    )(x2, g_mix, w_all, w_vt)


ML_T = 128
ML_HALO = 8


def _log_sigmoid(x):
    return jnp.minimum(x, 0.0) - jnp.log1p(jnp.exp(-jnp.abs(x)))


def _mlstm_kernel(qk_ref, halo_ref, mv_ref, mo_ref, gt_ref, cw_ref, cb_ref, gb_ref, gmh_ref,
                  hm_ref, xcat_ref, ct_ref, n_ref, m_ref):
    step = pl.program_id(0)

    @pl.when(step == 0)
    def _():
        ct_ref[...] = jnp.zeros_like(ct_ref)
        n_ref[...] = jnp.zeros_like(n_ref)
        m_ref[...] = jnp.zeros_like(m_ref)

    lane = lax.broadcasted_iota(jnp.int32, (M_CHUNK, LANES), 1)
    lo = lane < 64
    lane_row = lax.broadcasted_iota(jnp.int32, (1, LANES), 1) < 64
    r128 = lax.broadcasted_iota(jnp.int32, (LANES, LANES), 0)
    c128 = lax.broadcasted_iota(jnp.int32, (LANES, LANES), 1)
    tri_blk = ((r128 >= c128) & ((r128 < 64) == (c128 < 64))).astype(F32)
    top_rows = r128 < 64
    causal = (lax.broadcasted_iota(jnp.int32, (M_CHUNK, M_CHUNK), 0)
              >= lax.broadcasted_iota(jnp.int32, (M_CHUNK, M_CHUNK), 1))
    not_first = (step > 0).astype(F32)

    for b in range(BATCH):
        xcat_ref[b, 0:ML_HALO, :] = halo_ref[b] * not_first
        xcat_ref[b, ML_HALO:ML_HALO + ML_T, :] = qk_ref[b]
        acc = jnp.broadcast_to(cb_ref[...], (ML_T, 512))
        for j in range(M_CONV):
            acc = acc + cw_ref[j:j + 1, :] * xcat_ref[b, pl.ds(ML_HALO - (M_CONV - 1) + j, ML_T), :]
        qk = acc * jax.nn.sigmoid(acc)

        gb = gt_ref[b] + gb_ref[...]
        lf = _log_sigmoid(gb)
        bc = jnp.dot(tri_blk, lf, precision=lax.Precision.HIGHEST,
                     preferred_element_type=F32)
        b_t = bc.T
        i_t = gb.T

        for cc in range(ML_T // M_CHUNK):
            rows = slice(cc * M_CHUNK, (cc + 1) * M_CHUNK)
            for p in range(M_HEADS // 2):
                qp = qk[rows, p * 128:(p + 1) * 128] * (M_DQK ** -0.5)
                kp = qk[rows, 256 + p * 128:256 + (p + 1) * 128]
                kp_bf = kp.astype(BF16)
                sidx = b * 2 + p
                ct = ct_ref[sidx]
                ct_bf = ct.astype(BF16)
                nrow = n_ref[sidx][0:1, :]
                upd, decs, wcols = [], [], []
                for hh in range(2):
                    h = 2 * p + hh
                    mh = lo if hh == 0 else jnp.logical_not(lo)
                    qm = jnp.where(mh, qp, 0.0)
                    qm_bf = qm.astype(BF16)
                    m_prev = m_ref[b * M_HEADS + h][0:1, 0:1]
                    b_col = bc[rows, 4 + h:5 + h]
                    i_col = gb[rows, h:h + 1]
                    b_row = b_t[4 + h:5 + h, rows]
                    i_row = i_t[h:h + 1, rows]
                    log_d = jnp.where(causal, b_col - b_row + i_row, NEG_INF)
                    inter = b_col + m_prev
                    m_t = jnp.maximum(inter, jnp.max(log_d, axis=-1, keepdims=True))
                    dmat = jnp.exp(log_d - m_t)
                    s = _dot_nt(qm_bf, kp_bf) * dmat
                    w_inter = jnp.exp(inter - m_t)
                    vh = mv_ref[b, rows, h * M_DV:(h + 1) * M_DV]
                    num = _dot(s.astype(BF16), vh) + w_inter * _dot(qm_bf, ct_bf)
                    den = (jnp.sum(s, axis=-1, keepdims=True)
                           + w_inter * jnp.sum(qm * nrow, axis=-1, keepdims=True))
                    hv = num / jnp.maximum(jnp.abs(den), jnp.exp(-m_t))
                    m_new = m_t[M_CHUNK - 1:M_CHUNK, :]
                    b_last = b_col[M_CHUNK - 1:M_CHUNK, :]
                    w_col = jnp.exp(b_last - b_col + i_col - m_new)
                    decay = jnp.exp(b_last + m_prev - m_new)
                    wv = (w_col * vh.astype(F32)).astype(BF16)
                    upd.append(_dot_tn(kp_bf, wv))
                    decs.append(decay)
                    wcols.append(w_col)
                    m_ref[b * M_HEADS + h] = jnp.broadcast_to(m_new, (SUBLANES, LANES))
                    g_h = gmh_ref[:, h * M_DV:(h + 1) * M_DV]
                    gate = jax.nn.sigmoid(mo_ref[b, rows, h * M_DV:(h + 1) * M_DV])
                    hm_ref[b, rows, h * M_DV:(h + 1) * M_DV] = (gate * _rms(hv, g_h)).astype(BF16)
                dec_col = jnp.where(top_rows, decs[0], decs[1])
                ct_ref[sidx] = dec_col * ct + jnp.where(top_rows, upd[0], upd[1])
                dec_row = jnp.where(lane_row, decs[0], decs[1])
                wsel = jnp.where(lo, wcols[0], wcols[1])
                n_new = dec_row * nrow + jnp.sum(wsel * kp, axis=0, keepdims=True)
                n_ref[sidx] = jnp.broadcast_to(n_new, (SUBLANES, LANES))


def _mlstm(qk_pre, mv, mo, gates, conv_w, conv_b, gate_bias, g_mhead):
    nblk = SEQ // ML_T
    blk = lambda c: pl.BlockSpec((BATCH, ML_T, c), lambda i: (0, i, 0))
    full = lambda a: pl.BlockSpec(a.shape, lambda i: (0,) * a.ndim)
    halo = pl.BlockSpec((BATCH, ML_HALO, 512),
                        lambda i: (0, jnp.maximum(i * (ML_T // ML_HALO) - 1, 0), 0))
    return pl.pallas_call(
        _mlstm_kernel,
        grid=(nblk,),
        in_specs=[blk(512), halo, blk(512), blk(512), blk(128),
                  full(conv_w), full(conv_b), full(gate_bias), full(g_mhead)],
        out_specs=blk(512),
        out_shape=jax.ShapeDtypeStruct((BATCH, SEQ, 512), BF16),
        scratch_shapes=[
            pltpu.VMEM((BATCH, ML_HALO + ML_T, 512), F32),
            pltpu.VMEM((BATCH * 2, 128, 128), F32),
            pltpu.VMEM((BATCH * 2, SUBLANES, LANES), F32),
            pltpu.VMEM((BATCH * M_HEADS, SUBLANES, LANES), F32),
        ],
        compiler_params=_params(1, 32),
        name="mlstm",
    )(qk_pre, qk_pre, mv, mo, gates, conv_w, conv_b, gate_bias, g_mhead)


SWA_SUB = 4
SWA_T = SWA_SUB * A_BLOCK


def _swa_kernel(q_ref, kp_ref, kc_ref, vtp_ref, vtc_ref, sk_ref, o_ref):
    n = pl.program_id(1)
    lo = lax.broadcasted_iota(jnp.int32, (A_BLOCK, LANES), 1) < 64
    top = lax.broadcasted_iota(jnp.int32, (LANES, A_BLOCK), 0) < 64
    kj = lax.broadcasted_iota(jnp.int32, (2 * A_BLOCK, A_BLOCK), 0)
    qi = lax.broadcasted_iota(jnp.int32, (2 * A_BLOCK, A_BLOCK), 1)
    diff = qi - kj + A_BLOCK
    band = (diff >= 0) & (diff < WINDOW)
    band_first = band & ((kj >= A_BLOCK) | (n > 0))
    for sb in range(SWA_SUB):
        q_rows = slice(sb * A_BLOCK, (sb + 1) * A_BLOCK)
        mask = band_first if sb == 0 else band
        for j in range(A_KV_HEADS):
            cols = slice(j * 128, (j + 1) * 128)
            if sb == 0:
                kk = jnp.concatenate([kp_ref[0, :, cols], kc_ref[0, 0:A_BLOCK, cols]], axis=0)
                vt = jnp.concatenate([vtp_ref[cols, :], vtc_ref[cols, 0:A_BLOCK]], axis=1)
            else:
                kv_rows = slice((sb - 1) * A_BLOCK, (sb + 1) * A_BLOCK)
                kk = kc_ref[0, kv_rows, cols]
                vt = vtc_ref[cols, kv_rows]
            for p in range(2):
                c0 = j * 256 + p * 128
                qp = q_ref[0, q_rows, c0:c0 + 128].astype(F32)
                outs = []
                for hh in range(2):
                    h = j * 4 + p * 2 + hh
                    mh = lo if hh == 0 else jnp.logical_not(lo)
                    qm = jnp.where(mh, qp, 0.0).astype(BF16)
                    sc = jnp.where(mask, _dot_nt(kk, qm), NEG_INF)
                    sink = sk_ref[h:h + 1, 0:1]
                    mx = jnp.maximum(jnp.max(sc, axis=0, keepdims=True), sink)
                    pe = jnp.exp(sc - mx)
                    den = jnp.sum(pe, axis=0, keepdims=True) + jnp.exp(sink - mx)
                    outs.append(_dot(vt, pe.astype(BF16)) / den)
                o_t = jnp.where(top, outs[0], outs[1])
                o_ref[0, q_rows, c0:c0 + 128] = o_t.T.astype(BF16)


def _swa(aq, ak, av_t, sinks_b):
    per_b = SEQ // SWA_T
    cur = lambda c: pl.BlockSpec((1, SWA_T, c), lambda b, n: (b, n, 0))
    prev = lambda c: pl.BlockSpec((1, A_BLOCK, c), lambda b, n: (b, jnp.maximum(n * SWA_SUB - 1, 0), 0))
    vt_cur = pl.BlockSpec((256, SWA_T), lambda b, n: (0, b * per_b + n))
    vt_prev = pl.BlockSpec((256, A_BLOCK), lambda b, n: (0, b * per_b * SWA_SUB + jnp.maximum(n * SWA_SUB - 1, 0)))
    return pl.pallas_call(
        _swa_kernel,
        grid=(BATCH, per_b),
        in_specs=[cur(512), prev(256), cur(256), vt_prev, vt_cur,
                  pl.BlockSpec(sinks_b.shape, lambda b, n: (0, 0))],
        out_specs=cur(512),
        out_shape=jax.ShapeDtypeStruct((BATCH, SEQ, 512), BF16),
        compiler_params=_params(2, 32),
        name="swa",
    )(aq, ak, ak, av_t, av_t, sinks_b)


def _memkv_kernel(mem_ref, g_ref, wk_ref, wv_ref, k_ref, v_ref):
    mn = _rms(mem_ref[0], g_ref[...]).astype(BF16)
    k_ref[0] = _dot(mn, wk_ref[...]).astype(BF16)
    v_ref[0] = _dot(mn, wv_ref[...]).astype(BF16)


def _memkv(mem, g_mem, w_xk, w_xv):
    blk = pl.BlockSpec((1, MEM_LEN, D_MODEL), lambda b: (b, 0, 0))
    full = lambda a: pl.BlockSpec(a.shape, lambda b: (0,) * a.ndim)
    return pl.pallas_call(
        _memkv_kernel,
        grid=(BATCH,),
        in_specs=[blk, full(g_mem), full(w_xk), full(w_xv)],
        out_specs=[blk, blk],
        out_shape=[jax.ShapeDtypeStruct((BATCH, MEM_LEN, D_MODEL), BF16)] * 2,
        compiler_params=_params(1, 32),
        name="memkv",
    )(mem, g_mem, w_xk, w_xv)


POST_TM = 512


def _post_kernel(x_ref, hm_ref, ha_ref, wo_ref, gc_ref, wq_ref, km_ref, vm_ref, wxo_ref, gf_ref,
                 h2_ref, xn3_ref):
    h1 = x_ref[...] + _dot(hm_ref[...], wo_ref[0:512, :]) + _dot(ha_ref[...], wo_ref[512:1024, :])
    xn2 = _rms(h1, gc_ref[...]).astype(BF16)
    q = _dot(xn2, wq_ref[...]).astype(BF16)
    heads = []
    for hd in range(X_HEADS):
        cols = slice(hd * X_DH, (hd + 1) * X_DH)
        sc = _dot_nt(q[:, cols], km_ref[0, :, cols])
        mx = jnp.max(sc, axis=-1, keepdims=True)
        pe = jnp.exp(sc - mx)
        den = jnp.sum(pe, axis=-1, keepdims=True)
        heads.append((_dot(pe.astype(BF16), vm_ref[0, :, cols]) / den).astype(BF16))
    h2 = h1 + _dot(jnp.concatenate(heads, axis=1), wxo_ref[...])
    h2_ref[...] = h2
    xn3_ref[...] = _rms(h2, gf_ref[...]).astype(BF16)


def _post(x2, hm2, ha2, w_out, g_cross, w_xq, kmem, vmem, w_xo, g_ffn):
    n = x2.shape[0]
    per_batch = SEQ // POST_TM
    row = lambda c: pl.BlockSpec((POST_TM, c), lambda i: (i, 0))
    full = lambda a: pl.BlockSpec(a.shape, lambda i: (0,) * a.ndim)
    memblk = pl.BlockSpec((1, MEM_LEN, D_MODEL), lambda i: (i // per_batch, 0, 0))
    return pl.pallas_call(
        _post_kernel,
        grid=(n // POST_TM,),
        in_specs=[row(D_MODEL), row(512), row(512), full(w_out), full(g_cross), full(w_xq),
                  memblk, memblk, full(w_xo), full(g_ffn)],
        out_specs=[row(D_MODEL), row(D_MODEL)],
        out_shape=[jax.ShapeDtypeStruct((n, D_MODEL), F32), jax.ShapeDtypeStruct((n, D_MODEL), BF16)],
        compiler_params=_params(1, 48),
        name="post",
    )(x2, hm2, ha2, w_out, g_cross, w_xq, kmem, vmem, w_xo, g_ffn)


RT_T = 128


def _ce(vals, i, j):
    a, b = vals[i], vals[j]
    vals[i] = jnp.maximum(a, b)
    vals[j] = jnp.minimum(a, b)


def _bitonic_sort_desc(vals):
    n = len(vals)
    k = 2
    while k <= n:
        j = k // 2
        while j >= 1:
            for i in range(n):
                l = i ^ j
                if l > i:
                    if (i & k) == 0:
                        _ce(vals, i, l)
                    else:
                        _ce(vals, l, i)
            j //= 2
        k *= 2


def _bitonic_merge_desc(vals):
    n = len(vals)
    j = n // 2
    while j >= 1:
        for i in range(n):
            l = i ^ j
            if l > i:
                _ce(vals, i, l)
        j //= 2


def _merge_top16(xs, ys):
    m = [jnp.maximum(xs[a], ys[P_TOPK - 1 - a]) for a in range(P_TOPK)]
    _bitonic_merge_desc(m)
    return m


def _top16_over_keys(s_t):
    vals = [s_t[kb * SUBLANES:(kb + 1) * SUBLANES, :] for kb in range(P_NKEYS // SUBLANES)]
    _bitonic_sort_desc(vals)
    for sh in (4, 2, 1):
        rolled = [pltpu.roll(v, sh, axis=0) for v in vals]
        vals = _merge_top16(vals, rolled)
    return vals


def _route_kernel(xn_ref, wq_ref, k1_ref, k2_ref, cnt_ref, e1_ref, r2_ref, e2_ref):
    q_t = _dot_nt(wq_ref[...], xn_ref[...])
    sub = lax.broadcasted_iota(jnp.int32, (SUBLANES, RT_T), 0)
    s1_all, s2_all = [], []
    v1 = [None] * P_TOPK
    v2 = [None] * P_TOPK
    for h in range(P_HEADS):
        r0 = h * 2 * P_KEY_DIM
        s1 = jnp.dot(k1_ref[...], q_t[r0:r0 + P_KEY_DIM, :], precision=lax.Precision.HIGHEST,
                     preferred_element_type=F32)
        s2 = jnp.dot(k2_ref[...], q_t[r0 + P_KEY_DIM:r0 + 2 * P_KEY_DIM, :],
                     precision=lax.Precision.HIGHEST, preferred_element_type=F32)
        s1_all.append(s1)
        s2_all.append(s2)
        t1 = _top16_over_keys(s1)
        t2 = _top16_over_keys(s2)
        for a in range(P_TOPK):
            v1[a] = t1[a] if h == 0 else jnp.where(sub == h, t1[a], v1[a])
            v2[a] = t2[a] if h == 0 else jnp.where(sub == h, t2[a], v2[a])
    pairs = [(a, b) for a in range(P_TOPK) for b in range(P_TOPK // (a + 1))]
    cand_of = {ab: v1[ab[0]] + v2[ab[1]] for ab in pairs}
    pad = jnp.full((SUBLANES, RT_T), NEG_INF, F32)
    cands = [cand_of[ab] for ab in pairs] + [pad] * (64 - len(pairs))
    groups = []
    for gi in range(4):
        grp = cands[gi::4]
        _bitonic_sort_desc(grp)
        groups.append(grp)
    top = _merge_top16(_merge_top16(groups[0], groups[1]), _merge_top16(groups[2], groups[3]))
    tau = top[P_TOPK - 1]
    z = jnp.ones_like(tau)
    for kk in range(1, P_TOPK):
        z = z + jnp.exp(top[kk] - top[0])
    zinv = 1.0 / z
    cnt_rank = []
    for a in range(P_TOPK):
        c_a = jnp.zeros((SUBLANES, RT_T), F32)
        for b in range(P_TOPK // (a + 1)):
            c_a = jnp.where(cand_of[(a, b)] >= tau, float(b + 1), c_a)
        cnt_rank.append(c_a)
    for h in range(P_HEADS):
        s1, s2 = s1_all[h], s2_all[h]
        cnt_rows = jnp.zeros((P_NKEYS, RT_T), F32)
        for a in range(P_TOPK):
            cnt_rows = jnp.where(s1 == v1[a][h:h + 1, :], cnt_rank[a][h:h + 1, :], cnt_rows)
        rank2 = jnp.zeros((P_NKEYS, RT_T), F32)
        for b in range(P_TOPK):
            rank2 = jnp.where(s2 < v2[b][h:h + 1, :], float(b + 1), rank2)
        cnt_ref[h] = cnt_rows
        e1_ref[h] = jnp.exp(s1 - v1[0][h:h + 1, :]) * zinv[h:h + 1, :]
        r2_ref[h] = pltpu.bitcast(rank2.astype(BF16), jnp.uint32)
        e2_ref[h] = pltpu.bitcast(jnp.exp(s2 - v2[0][h:h + 1, :]).astype(BF16), jnp.uint32)


def _route(xn3, w_pq_t, keys1, keys2):
    n = xn3.shape[0]
    full = lambda a: pl.BlockSpec(a.shape, lambda i: (0,) * a.ndim)
    oblk = lambda rows: pl.BlockSpec((P_HEADS, rows, RT_T), lambda i: (0, 0, i))
    f32_rows = jax.ShapeDtypeStruct((P_HEADS, P_NKEYS, n), F32)
    packed_rows = jax.ShapeDtypeStruct((P_HEADS, P_NKEYS // 2, n), jnp.uint32)
    return pl.pallas_call(
        _route_kernel,
        grid=(n // RT_T,),
        in_specs=[pl.BlockSpec((RT_T, D_MODEL), lambda i: (i, 0)), full(w_pq_t), full(keys1), full(keys2)],
        out_specs=[oblk(P_NKEYS), oblk(P_NKEYS), oblk(P_NKEYS // 2), oblk(P_NKEYS // 2)],
        out_shape=[f32_rows, f32_rows, packed_rows, packed_rows],
        compiler_params=_params(1, 32),
        name="route",
    )(xn3, w_pq_t, keys1, keys2)


PE_T = 512
PE_E = 1024
_SQRT_HALF = 0.7071067811865476


PE_NE = P_EXPERTS // PE_E
PE_ROWS = PE_E // P_NKEYS


def _peer_kernel(dn_ref, up_ref, xn_ref, cnt_ref, e1_ref, r2_ref, e2_ref, o_ref, a_buf, g_buf):
    e = pl.program_id(1)
    slot = lax.rem(e, 2)
    other = 1 - slot

    @pl.when(e == 0)
    def _():
        o_ref[...] = jnp.zeros_like(o_ref)
        a_buf[1] = jnp.zeros((PE_E, PE_T), F32)
        g_buf[0] = jnp.zeros((PE_E, PE_T), BF16)

    a_buf[slot] = _dot_nt(dn_ref[...], xn_ref[...])
    o_ref[...] += _dot(up_ref[...], g_buf[slot])

    assert PE_ROWS == SUBLANES
    ig0 = pl.multiple_of(jnp.clip(e - 1, 0, PE_NE - 1) * PE_ROWS, SUBLANES)
    for c in range(PE_T // LANES):
        cs = slice(c * LANES, (c + 1) * LANES)
        cnt8 = [cnt_ref[h, pl.ds(ig0, SUBLANES), cs] for h in range(P_HEADS)]
        e18 = [e1_ref[h, pl.ds(ig0, SUBLANES), cs] for h in range(P_HEADS)]
        for i in range(PE_ROWS):
            cnt_t = [jnp.broadcast_to(cnt8[h][i:i + 1, :], (BF16_ROWS, LANES)).astype(BF16) for h in range(P_HEADS)]
            e1_t = [jnp.broadcast_to(e18[h][i:i + 1, :], (BF16_ROWS, LANES)).astype(BF16) for h in range(P_HEADS)]
            for k in range(P_NKEYS // BF16_ROWS):
                js = slice(k * SUBLANES, (k + 1) * SUBLANES)
                rows = slice(i * P_NKEYS + k * BF16_ROWS, i * P_NKEYS + (k + 1) * BF16_ROWS)
                coef = jnp.zeros((BF16_ROWS, LANES), BF16)
                for h in range(P_HEADS):
                    r2 = pltpu.bitcast(r2_ref[h, js, cs], BF16)
                    e2 = pltpu.bitcast(e2_ref[h, js, cs], BF16)
                    coef = coef + jnp.where(r2 < cnt_t[h], e2 * e1_t[h], jnp.zeros((), BF16))
                a = a_buf[other, rows, cs]
                gl = 0.5 * a * (1.0 + lax.erf(a * _SQRT_HALF))
                g_buf[other, rows, cs] = gl.astype(BF16) * coef


def _peer(dn, up_t, xn3, cnt, e1, r2, e2):
    n = xn3.shape[0]
    rblk = pl.BlockSpec((P_HEADS, P_NKEYS, PE_T), lambda t, e: (0, 0, t))
    pblk = pl.BlockSpec((P_HEADS, P_NKEYS // 2, PE_T), lambda t, e: (0, 0, t))
    return pl.pallas_call(
        _peer_kernel,
        grid=(n // PE_T, PE_NE + 2),
        in_specs=[pl.BlockSpec((PE_E, D_MODEL), lambda t, e: (jnp.minimum(e, PE_NE - 1), 0)),
                  pl.BlockSpec((D_MODEL, PE_E), lambda t, e: (0, jnp.clip(e - 2, 0, PE_NE - 1))),
                  pl.BlockSpec((PE_T, D_MODEL), lambda t, e: (t, 0)),
                  rblk, rblk, pblk, pblk],
        out_specs=pl.BlockSpec((D_MODEL, PE_T), lambda t, e: (0, t)),
        out_shape=jax.ShapeDtypeStruct((D_MODEL, n), F32),
        scratch_shapes=[pltpu.VMEM((2, PE_E, PE_T), F32), pltpu.VMEM((2, PE_E, PE_T), BF16)],
        compiler_params=_params(2, 48),
        name="peer",
    )(dn, up_t, xn3, cnt, e1, r2, e2)


FIN_TM = 512


def _final_kernel(h_ref, pt_ref, g_ref, o_ref):
    o_ref[...] = _rms(h_ref[...] + pt_ref[...].T, g_ref[...])


def _final(h2, peer_t, g_final):
    n = h2.shape[0]
    return pl.pallas_call(
        _final_kernel,
        grid=(n // FIN_TM,),
        in_specs=[pl.BlockSpec((FIN_TM, D_MODEL), lambda i: (i, 0)),
                  pl.BlockSpec((D_MODEL, FIN_TM), lambda i: (0, i)),
                  pl.BlockSpec(g_final.shape, lambda i: (0, 0))],
        out_specs=pl.BlockSpec((FIN_TM, D_MODEL), lambda i: (i, 0)),
        out_shape=jax.ShapeDtypeStruct((n, D_MODEL), F32),
        compiler_params=_params(1, 32),
        name="final",
    )(h2, peer_t, g_final)


def _pack_w_in(w):
    o = 0
    segs = {}
    for name, sz in (("mq", 256), ("mk", 256), ("mv", 512), ("mo", 512), ("mi", 4), ("mf", 4),
                     ("aq", 512), ("ak", 128), ("av", 128)):
        segs[name] = w[:, o:o + sz]
        o += sz
    k0, k1 = segs["ak"][:, :64], segs["ak"][:, 64:]
    v0, v1 = segs["av"][:, :64], segs["av"][:, 64:]
    gates = jnp.concatenate([segs["mi"], segs["mf"], jnp.zeros((w.shape[0], LANES - 8), w.dtype)], axis=1)
    packed = jnp.concatenate([segs["mq"], segs["mk"], segs["mv"], segs["mo"],
                              segs["aq"] * (A_DH ** -0.5), k0, k0, k1, k1, gates], axis=1)
    w_vt = jnp.concatenate([v0, v0, v1, v1], axis=1).T
    return packed.astype(BF16), w_vt.astype(BF16)


def kernel(x, mem, g_mix, w_in, conv_w, conv_b, b_igate, b_fgate, g_mhead, sinks, w_out, g_cross, g_mem,
           w_xq, w_xk, w_xv, w_xo, g_ffn, w_pq, sub_keys1, sub_keys2, expert_down, expert_up, g_final):
    assert x.shape == (BATCH, SEQ, D_MODEL) and w_in.shape[0] == 1
    l = 0
    row = lambda v: v.reshape(1, -1).astype(F32)
    x2 = x.reshape(NTOK, D_MODEL)

    qk_pre, mv, mo, aq, ak, av_t, gates = _mix_in(x2, row(g_mix[l]), *_pack_w_in(w_in[l]))

    gate_bias = jnp.concatenate([b_igate[l], b_fgate[l], jnp.zeros((LANES - 8,), F32)]).reshape(1, LANES)
    r3 = lambda a: a.reshape(BATCH, SEQ, a.shape[-1])
    hm = _mlstm(r3(qk_pre), r3(mv), r3(mo), r3(gates), conv_w[l], row(conv_b[l]), gate_bias, row(g_mhead[l]))

    sinks_b = jnp.broadcast_to(sinks[l].astype(F32)[:, None], (A_HEADS, LANES))
    ha = _swa(r3(aq), r3(ak), av_t, sinks_b)

    kmem, vmem = _memkv(mem, row(g_mem[l]), w_xk[l].astype(BF16), w_xv[l].astype(BF16))

    h2, xn3 = _post(x2, hm.reshape(NTOK, 512), ha.reshape(NTOK, 512), w_out[l].astype(BF16), row(g_cross[l]),
                    (w_xq[l] * (X_DH ** -0.5)).astype(BF16), kmem, vmem, w_xo[l].astype(BF16), row(g_ffn[l]))

    cnt, e1, r2, e2 = _route(xn3, w_pq[l].T.astype(BF16), sub_keys1[l], sub_keys2[l])

    peer_t = _peer(expert_down[l].astype(BF16), expert_up[l].T.astype(BF16), xn3, cnt, e1, r2, e2)

    out = _final(h2, peer_t, row(g_final))
    return out.reshape(BATCH, SEQ, D_MODEL)
```

```python
import functools

import jax
import jax.numpy as jnp
from jax import lax
from jax.experimental import pallas as pl
from jax.experimental.pallas import tpu as pltpu

F32 = jnp.float32
BF16 = jnp.bfloat16

D_MODEL = 1024
BATCH = 2
SEQ = 8192
NTOK = BATCH * SEQ
MEM_LEN = 256
EPS = 1e-6

M_HEADS = 4
M_DV = 128
M_DQK = 64
M_CONV = 4
M_CHUNK = 64

A_HEADS = 8
A_KV_HEADS = 2
A_DH = 64
WINDOW = 128
A_BLOCK = 128

X_HEADS = 4
X_DH = 256

P_HEADS = 8
P_NKEYS = 128
P_EXPERTS = P_NKEYS * P_NKEYS
P_KEY_DIM = 128
P_TOPK = 16

LANES = 128
SUBLANES = 8
BF16_ROWS = 2 * SUBLANES

NEG_INF = float("-inf")

V7X_VMEM_BYTES = 64 * 1024 * 1024
MIB = 1024 * 1024


def _params(ndims, vmem_mib):
    assert vmem_mib * MIB < V7X_VMEM_BYTES
    return pltpu.CompilerParams(dimension_semantics=("arbitrary",) * ndims, vmem_limit_bytes=vmem_mib * MIB)


def _rms(xf, g):
    return xf * lax.rsqrt(jnp.mean(xf * xf, axis=-1, keepdims=True) + EPS) * g


def _dot_nt(a, b):
    return lax.dot_general(a, b, (((1,), (1,)), ((), ())), preferred_element_type=F32)


def _dot_tn(a, b):
    return lax.dot_general(a, b, (((0,), (0,)), ((), ())), preferred_element_type=F32)


def _dot(a, b):
    return jnp.dot(a, b, preferred_element_type=F32)


MIX_TM = 512


def _mix_in_kernel(x_ref, g_ref, w_ref, wt_ref, qk_ref, aq_ref, ak_ref, gt_ref, mvt_ref, mot_ref, avt_ref):
    xn = _rms(x_ref[...], g_ref[...]).astype(BF16)
    qk_ref[...] = _dot(xn, w_ref[:, 0:512])
    aq_ref[...] = _dot(xn, w_ref[:, 512:1024]).astype(BF16)
    ak_ref[...] = _dot(xn, w_ref[:, 1024:1280]).astype(BF16)
    gt_ref[...] = _dot(xn, w_ref[:, 1280:1408])
    mvt_ref[...] = _dot_nt(wt_ref[0:512, :], xn).astype(BF16)
    mot_ref[...] = _dot_nt(wt_ref[512:1024, :], xn)
    avt_ref[...] = _dot_nt(wt_ref[1024:1280, :], xn).astype(BF16)


def _mix_in(x2, g_mix, w_all, w_t):
    n = x2.shape[0]
    grid = (n // MIX_TM,)
    row = lambda c: pl.BlockSpec((MIX_TM, c), lambda i: (i, 0))
    col = lambda r: pl.BlockSpec((r, MIX_TM), lambda i: (0, i))
    full = lambda a: pl.BlockSpec(a.shape, lambda i: (0,) * a.ndim)
    return pl.pallas_call(
        _mix_in_kernel,
        grid=grid,
        in_specs=[row(D_MODEL), full(g_mix), full(w_all), full(w_t)],
        out_specs=[row(512), row(512), row(256), row(128), col(512), col(512), col(256)],
        out_shape=[
            jax.ShapeDtypeStruct((n, 512), F32),
            jax.ShapeDtypeStruct((n, 512), BF16),
            jax.ShapeDtypeStruct((n, 256), BF16),
            jax.ShapeDtypeStruct((n, 128), F32),
            jax.ShapeDtypeStruct((512, n), BF16),
            jax.ShapeDtypeStruct((512, n), F32),
            jax.ShapeDtypeStruct((256, n), BF16),
        ],
        compiler_params=_params(1, 40),
        name="mix_in",
    )(x2, g_mix, w_all, w_t)


ML_T = 128
ML_HALO = 8


def _log_sigmoid(x):
    return jnp.minimum(x, 0.0) - jnp.log1p(jnp.exp(-jnp.abs(x)))


def _mlstm_kernel(qk_ref, halo_ref, mvt0_ref, mvt1_ref, mot0_ref, mot1_ref, gt_ref, cw_ref, cb_ref, gb_ref,
                  gmh_ref, hm_ref, xcat_ref, c_ref, n_ref, m_ref):
    step = pl.program_id(0)
    mvt_refs = (mvt0_ref, mvt1_ref)
    mot_refs = (mot0_ref, mot1_ref)

    @pl.when(step == 0)
    def _():
        c_ref[...] = jnp.zeros_like(c_ref)
        n_ref[...] = jnp.zeros_like(n_ref)
        m_ref[...] = jnp.zeros_like(m_ref)

    src = lax.broadcasted_iota(jnp.int32, (ML_T, ML_T), 0)
    dst = lax.broadcasted_iota(jnp.int32, (ML_T, ML_T), 1)
    lo = dst < 64
    lo_row = lax.broadcasted_iota(jnp.int32, (1, LANES), 1) < 64
    causal = src <= dst
    tri = (src >= dst).astype(F32)
    not_first = (step > 0).astype(F32)

    for b in range(BATCH):
        mvt_ref, mot_ref = mvt_refs[b], mot_refs[b]
        xcat_ref[b, 0:ML_HALO, :] = halo_ref[b] * not_first
        xcat_ref[b, ML_HALO:ML_HALO + ML_T, :] = qk_ref[b]
        acc = jnp.broadcast_to(cb_ref[...], (ML_T, 512))
        for j in range(M_CONV):
            acc = acc + cw_ref[j:j + 1, :] * xcat_ref[b, pl.ds(ML_HALO - (M_CONV - 1) + j, ML_T), :]
        qk = acc * jax.nn.sigmoid(acc)

        gb = gt_ref[b] + gb_ref[...]
        lf = _log_sigmoid(gb)
        bc = jnp.dot(tri, lf, precision=lax.Precision.HIGHEST,
                     preferred_element_type=F32)
        b_t = bc.T

        for p in range(M_HEADS // 2):
            qp = qk[:, p * 128:(p + 1) * 128] * (M_DQK ** -0.5)
            kp = qk[:, 256 + p * 128:256 + (p + 1) * 128]
            kp_bf = kp.astype(BF16)
            nrow = n_ref[b * 2 + p][0:1, :]
            n_bf = jnp.broadcast_to(nrow, (SUBLANES, LANES)).astype(BF16)
            n_parts = []
            for hh in range(2):
                h = 2 * p + hh
                sidx = b * M_HEADS + h
                mh = lo if hh == 0 else jnp.logical_not(lo)
                qm_bf = jnp.where(mh, qp, 0.0).astype(BF16)
                s_t = _dot_nt(kp_bf, qm_bf)
                m_prev = m_ref[sidx][0:1, 0:1]
                b_row = b_t[4 + h:5 + h, :]
                col = jnp.broadcast_to(gb[:, h:h + 1] - bc[:, 4 + h:5 + h], (ML_T, ML_T))
                log_d = jnp.where(causal, b_row + col, NEG_INF)
                inter = b_row + m_prev
                m_t = jnp.maximum(inter, jnp.max(log_d, axis=0, keepdims=True))
                sp = s_t * jnp.exp(log_d - m_t)
                w_inter = jnp.exp(inter - m_t)
                qn = _dot_nt(n_bf, qm_bf)[0:1, :]
                den = jnp.sum(sp, axis=0, keepdims=True) + w_inter * qn
                v_t = mvt_ref[h * M_DV:(h + 1) * M_DV, :]
                c_h = c_ref[sidx]
                num_t = _dot(v_t, sp.astype(BF16)) + w_inter * _dot_nt(c_h.astype(BF16), qm_bf)
                h_t = num_t / jnp.maximum(jnp.abs(den), jnp.exp(-m_t))
                h_n = h_t * lax.rsqrt(jnp.mean(h_t * h_t, axis=0, keepdims=True) + EPS)
                rows_h = slice(h * M_DV, (h + 1) * M_DV)
                out_t = jax.nn.sigmoid(mot_ref[rows_h, :]) * (h_n * gmh_ref[rows_h, :])
                hm_ref[b, :, h * M_DV:(h + 1) * M_DV] = out_t.T.astype(BF16)
                m_new = m_t[:, ML_T - 1:ML_T]
                b_last = b_row[:, ML_T - 1:ML_T]
                decay = jnp.exp(b_last + m_prev - m_new)
                wk = jnp.exp(col + (b_last - m_new)) * kp
                c_ref[sidx] = decay * c_h + jnp.where(mh, _dot(v_t, wk.astype(BF16)), 0.0)
                n_parts.append(decay * nrow + jnp.sum(wk, axis=0, keepdims=True))
                m_ref[sidx] = jnp.broadcast_to(m_new, (SUBLANES, LANES))
            n_new = jnp.where(lo_row, n_parts[0], n_parts[1])
            n_ref[b * 2 + p] = jnp.broadcast_to(n_new, (SUBLANES, LANES))


def _mlstm(qk_pre, mv_t, mo_t, gates, conv_w, conv_b, gate_bias, g_mhead_b):
    nblk = SEQ // ML_T
    blk = lambda c: pl.BlockSpec((BATCH, ML_T, c), lambda i: (0, i, 0))
    full = lambda a: pl.BlockSpec(a.shape, lambda i: (0,) * a.ndim)
    halo = pl.BlockSpec((BATCH, ML_HALO, 512),
                        lambda i: (0, jnp.maximum(i * (ML_T // ML_HALO) - 1, 0), 0))
    t_blk = lambda b: pl.BlockSpec((512, ML_T), lambda i: (0, b * nblk + i))
    return pl.pallas_call(
        _mlstm_kernel,
        grid=(nblk,),
        in_specs=[blk(512), halo, t_blk(0), t_blk(1), t_blk(0), t_blk(1), blk(128),
                  full(conv_w), full(conv_b), full(gate_bias), full(g_mhead_b)],
        out_specs=blk(512),
        out_shape=jax.ShapeDtypeStruct((BATCH, SEQ, 512), BF16),
        scratch_shapes=[
            pltpu.VMEM((BATCH, ML_HALO + ML_T, 512), F32),
            pltpu.VMEM((BATCH * M_HEADS, M_DV, LANES), F32),
            pltpu.VMEM((BATCH * 2, SUBLANES, LANES), F32),
            pltpu.VMEM((BATCH * M_HEADS, SUBLANES, LANES), F32),
        ],
        compiler_params=_params(1, 32),
        name="mlstm",
    )(qk_pre, qk_pre, mv_t, mv_t, mo_t, mo_t, gates, conv_w, conv_b, gate_bias, g_mhead_b)


SWA_SUB = 4
SWA_T = SWA_SUB * A_BLOCK


def _swa_kernel(q_ref, kp_ref, kc_ref, vtp_ref, vtc_ref, sk_ref, o_ref):
    n = pl.program_id(1)
    lo = lax.broadcasted_iota(jnp.int32, (A_BLOCK, LANES), 1) < 64
    top = lax.broadcasted_iota(jnp.int32, (LANES, A_BLOCK), 0) < 64
    kj = lax.broadcasted_iota(jnp.int32, (2 * A_BLOCK, A_BLOCK), 0)
    qi = lax.broadcasted_iota(jnp.int32, (2 * A_BLOCK, A_BLOCK), 1)
    diff = qi - kj + A_BLOCK
    band = (diff >= 0) & (diff < WINDOW)
    band_first = band & ((kj >= A_BLOCK) | (n > 0))
    for sb in range(SWA_SUB):
        q_rows = slice(sb * A_BLOCK, (sb + 1) * A_BLOCK)
        mask = band_first if sb == 0 else band
        for j in range(A_KV_HEADS):
            cols = slice(j * 128, (j + 1) * 128)
            if sb == 0:
                kk = jnp.concatenate([kp_ref[0, :, cols], kc_ref[0, 0:A_BLOCK, cols]], axis=0)
                vt = jnp.concatenate([vtp_ref[cols, :], vtc_ref[cols, 0:A_BLOCK]], axis=1)
            else:
                kv_rows = slice((sb - 1) * A_BLOCK, (sb + 1) * A_BLOCK)
                kk = kc_ref[0, kv_rows, cols]
                vt = vtc_ref[cols, kv_rows]
            for p in range(2):
                c0 = j * 256 + p * 128
                qp = q_ref[0, q_rows, c0:c0 + 128].astype(F32)
                outs = []
                for hh in range(2):
                    h = j * 4 + p * 2 + hh
                    mh = lo if hh == 0 else jnp.logical_not(lo)
                    qm = jnp.where(mh, qp, 0.0).astype(BF16)
                    sc = jnp.where(mask, _dot_nt(kk, qm), NEG_INF)
                    sink = sk_ref[h:h + 1, 0:1]
                    mx = jnp.maximum(jnp.max(sc, axis=0, keepdims=True), sink)
                    pe = jnp.exp(sc - mx)
                    den = jnp.sum(pe, axis=0, keepdims=True) + jnp.exp(sink - mx)
                    outs.append(_dot(vt, pe.astype(BF16)) / den)
                o_t = jnp.where(top, outs[0], outs[1])
                o_ref[0, q_rows, c0:c0 + 128] = o_t.T.astype(BF16)


def _swa(aq, ak, av_t, sinks_b):
    per_b = SEQ // SWA_T
    cur = lambda c: pl.BlockSpec((1, SWA_T, c), lambda b, n: (b, n, 0))
    prev = lambda c: pl.BlockSpec((1, A_BLOCK, c), lambda b, n: (b, jnp.maximum(n * SWA_SUB - 1, 0), 0))
    vt_cur = pl.BlockSpec((256, SWA_T), lambda b, n: (0, b * per_b + n))
    vt_prev = pl.BlockSpec((256, A_BLOCK), lambda b, n: (0, b * per_b * SWA_SUB + jnp.maximum(n * SWA_SUB - 1, 0)))
    return pl.pallas_call(
        _swa_kernel,
        grid=(BATCH, per_b),
        in_specs=[cur(512), prev(256), cur(256), vt_prev, vt_cur,
                  pl.BlockSpec(sinks_b.shape, lambda b, n: (0, 0))],
        out_specs=cur(512),
        out_shape=jax.ShapeDtypeStruct((BATCH, SEQ, 512), BF16),
        compiler_params=_params(2, 32),
        name="swa",
    )(aq, ak, ak, av_t, av_t, sinks_b)


def _memkv_kernel(mem_ref, g_ref, wk_ref, wv_ref, k_ref, v_ref):
    mn = _rms(mem_ref[0], g_ref[...]).astype(BF16)
    k_ref[0] = _dot(mn, wk_ref[...]).astype(BF16)
    v_ref[0] = _dot(mn, wv_ref[...]).astype(BF16)


def _memkv(mem, g_mem, w_xk, w_xv):
    blk = pl.BlockSpec((1, MEM_LEN, D_MODEL), lambda b: (b, 0, 0))
    full = lambda a: pl.BlockSpec(a.shape, lambda b: (0,) * a.ndim)
    return pl.pallas_call(
        _memkv_kernel,
        grid=(BATCH,),
        in_specs=[blk, full(g_mem), full(w_xk), full(w_xv)],
        out_specs=[blk, blk],
        out_shape=[jax.ShapeDtypeStruct((BATCH, MEM_LEN, D_MODEL), BF16)] * 2,
        compiler_params=_params(1, 32),
        name="memkv",
    )(mem, g_mem, w_xk, w_xv)


POST_TM = 512


def _post_kernel(x_ref, hm_ref, ha_ref, wo_ref, gc_ref, wq_ref, km_ref, vm_ref, wxo_ref, gf_ref,
                 h2_ref, xn3_ref):
    h1 = x_ref[...] + _dot(hm_ref[...], wo_ref[0:512, :]) + _dot(ha_ref[...], wo_ref[512:1024, :])
    xn2 = _rms(h1, gc_ref[...]).astype(BF16)
    q = _dot(xn2, wq_ref[...]).astype(BF16)
    heads = []
    for hd in range(X_HEADS):
        cols = slice(hd * X_DH, (hd + 1) * X_DH)
        sc = _dot_nt(q[:, cols], km_ref[0, :, cols])
        mx = jnp.max(sc, axis=-1, keepdims=True)
        pe = jnp.exp(sc - mx)
        den = jnp.sum(pe, axis=-1, keepdims=True)
        heads.append((_dot(pe.astype(BF16), vm_ref[0, :, cols]) / den).astype(BF16))
    h2 = h1 + _dot(jnp.concatenate(heads, axis=1), wxo_ref[...])
    h2_ref[...] = h2
    xn3_ref[...] = _rms(h2, gf_ref[...]).astype(BF16)


def _post(x2, hm2, ha2, w_out, g_cross, w_xq, kmem, vmem, w_xo, g_ffn):
    n = x2.shape[0]
    per_batch = SEQ // POST_TM
    row = lambda c: pl.BlockSpec((POST_TM, c), lambda i: (i, 0))
    full = lambda a: pl.BlockSpec(a.shape, lambda i: (0,) * a.ndim)
    memblk = pl.BlockSpec((1, MEM_LEN, D_MODEL), lambda i: (i // per_batch, 0, 0))
    return pl.pallas_call(
        _post_kernel,
        grid=(n // POST_TM,),
        in_specs=[row(D_MODEL), row(512), row(512), full(w_out), full(g_cross), full(w_xq),
                  memblk, memblk, full(w_xo), full(g_ffn)],
        out_specs=[row(D_MODEL), row(D_MODEL)],
        out_shape=[jax.ShapeDtypeStruct((n, D_MODEL), F32), jax.ShapeDtypeStruct((n, D_MODEL), BF16)],
        compiler_params=_params(1, 48),
        name="post",
    )(x2, hm2, ha2, w_out, g_cross, w_xq, kmem, vmem, w_xo, g_ffn)


RT_T = 128


def _ce(vals, i, j):
    a, b = vals[i], vals[j]
    vals[i] = jnp.maximum(a, b)
    vals[j] = jnp.minimum(a, b)


def _bitonic_sort_desc(vals):
    n = len(vals)
    k = 2
    while k <= n:
        j = k // 2
        while j >= 1:
            for i in range(n):
                l = i ^ j
                if l > i:
                    if (i & k) == 0:
                        _ce(vals, i, l)
                    else:
                        _ce(vals, l, i)
            j //= 2
        k *= 2


def _bitonic_merge_desc(vals):
    n = len(vals)
    j = n // 2
    while j >= 1:
        for i in range(n):
            l = i ^ j
            if l > i:
                _ce(vals, i, l)
        j //= 2


def _merge_top16(xs, ys):
    m = [jnp.maximum(xs[a], ys[P_TOPK - 1 - a]) for a in range(P_TOPK)]
    _bitonic_merge_desc(m)
    return m


def _top16_over_keys(vals):
    vals = list(vals)
    _bitonic_sort_desc(vals)
    for sh in (4, 2, 1):
        rolled = [pltpu.roll(v, sh, axis=0) for v in vals]
        vals = _merge_top16(vals, rolled)
    return vals


def _route_kernel(xn_ref, wq_ref, k1_ref, k2_ref, cnt_ref, e1_ref, r2_ref, e2_ref):
    q_t = _dot_nt(wq_ref[...], xn_ref[...])
    s1_all, s2_all = [], []
    for h in range(P_HEADS):
        r0 = h * 2 * P_KEY_DIM
        s1_all.append(jnp.dot(k1_ref[...], q_t[r0:r0 + P_KEY_DIM, :], precision=lax.Precision.HIGHEST,
                              preferred_element_type=F32))
        s2_all.append(jnp.dot(k2_ref[...], q_t[r0 + P_KEY_DIM:r0 + 2 * P_KEY_DIM, :],
                              precision=lax.Precision.HIGHEST, preferred_element_type=F32))

    probs = [s for pair in zip(s1_all, s2_all) for s in pair]
    wide = [jnp.concatenate([s[kb * SUBLANES:(kb + 1) * SUBLANES, :] for s in probs], axis=1)
            for kb in range(P_NKEYS // SUBLANES)]
    tops = _top16_over_keys(wide)
    sub = lax.broadcasted_iota(jnp.int32, (SUBLANES, RT_T), 0)
    v1 = [None] * P_TOPK
    v2 = [None] * P_TOPK
    for a in range(P_TOPK):
        for h in range(P_HEADS):
            t1 = tops[a][:, (2 * h) * RT_T:(2 * h + 1) * RT_T]
            t2 = tops[a][:, (2 * h + 1) * RT_T:(2 * h + 2) * RT_T]
            v1[a] = t1 if h == 0 else jnp.where(sub == h, t1, v1[a])
            v2[a] = t2 if h == 0 else jnp.where(sub == h, t2, v2[a])
    pairs = [(a, b) for a in range(P_TOPK) for b in range(P_TOPK // (a + 1))]
    cand_of = {ab: v1[ab[0]] + v2[ab[1]] for ab in pairs}
    pad = jnp.full((SUBLANES, RT_T), NEG_INF, F32)
    cands = [cand_of[ab] for ab in pairs] + [pad] * (64 - len(pairs))
    groups = []
    for gi in range(4):
        grp = cands[gi::4]
        _bitonic_sort_desc(grp)
        groups.append(grp)
    top = _merge_top16(_merge_top16(groups[0], groups[1]), _merge_top16(groups[2], groups[3]))
    tau = top[P_TOPK - 1]
    z = jnp.ones_like(tau)
    for kk in range(1, P_TOPK):
        z = z + jnp.exp(top[kk] - top[0])
    zinv = 1.0 / z
    cnt_rank = []
    for a in range(P_TOPK):
        c_a = jnp.zeros((SUBLANES, RT_T), F32)
        for b in range(P_TOPK // (a + 1)):
            c_a = jnp.where(cand_of[(a, b)] >= tau, float(b + 1), c_a)
        cnt_rank.append(c_a)
    for h in range(P_HEADS):
        s1, s2 = s1_all[h], s2_all[h]
        cnt_rows = jnp.zeros((P_NKEYS, RT_T), F32)
        for a in range(P_TOPK):
            cnt_rows = jnp.where(s1 == v1[a][h:h + 1, :], cnt_rank[a][h:h + 1, :], cnt_rows)
        rank2 = jnp.zeros((P_NKEYS, RT_T), F32)
        for b in range(P_TOPK):
            rank2 = jnp.where(s2 < v2[b][h:h + 1, :], float(b + 1), rank2)
        cnt_ref[h] = cnt_rows
        e1_ref[h] = jnp.exp(s1 - v1[0][h:h + 1, :]) * zinv[h:h + 1, :]
        r2_ref[h] = pltpu.bitcast(rank2.astype(BF16), jnp.uint32)
        e2_ref[h] = pltpu.bitcast(jnp.exp(s2 - v2[0][h:h + 1, :]).astype(BF16), jnp.uint32)


def _route(xn3, w_pq_t, keys1, keys2):
    n = xn3.shape[0]
    full = lambda a: pl.BlockSpec(a.shape, lambda i: (0,) * a.ndim)
    oblk = lambda rows: pl.BlockSpec((P_HEADS, rows, RT_T), lambda i: (0, 0, i))
    f32_rows = jax.ShapeDtypeStruct((P_HEADS, P_NKEYS, n), F32)
    packed_rows = jax.ShapeDtypeStruct((P_HEADS, P_NKEYS // 2, n), jnp.uint32)
    return pl.pallas_call(
        _route_kernel,
        grid=(n // RT_T,),
        in_specs=[pl.BlockSpec((RT_T, D_MODEL), lambda i: (i, 0)), full(w_pq_t), full(keys1), full(keys2)],
        out_specs=[oblk(P_NKEYS), oblk(P_NKEYS), oblk(P_NKEYS // 2), oblk(P_NKEYS // 2)],
        out_shape=[f32_rows, f32_rows, packed_rows, packed_rows],
        compiler_params=_params(1, 32),
        name="route",
    )(xn3, w_pq_t, keys1, keys2)


PE_T = 1024
PE_E = 1024
_SQRT_HALF = 0.7071067811865476


PE_NE = P_EXPERTS // PE_E
PE_ROWS = PE_E // P_NKEYS


def _peer_kernel(dn_ref, up_ref, xn_ref, cnt_ref, e1_ref, r2_ref, e2_ref, o_ref, a_buf, g_buf):
    e = pl.program_id(1)
    slot = lax.rem(e, 2)
    other = 1 - slot

    @pl.when(e == 0)
    def _():
        o_ref[...] = jnp.zeros_like(o_ref)
        a_buf[1] = jnp.zeros((PE_E, PE_T), F32)
        g_buf[0] = jnp.zeros((PE_E, PE_T), BF16)

    a_buf[slot] = _dot_nt(dn_ref[...], xn_ref[...])
    o_ref[...] += _dot(up_ref[...], g_buf[slot])

    for c in range(PE_T // LANES):
        cs = slice(c * LANES, (c + 1) * LANES)
        n_slab = P_NKEYS // BF16_ROWS
        for i in range(PE_ROWS):
            coef = [jnp.zeros((BF16_ROWS, LANES), BF16)] * n_slab
            for h in range(P_HEADS):
                cnt_t = jnp.broadcast_to(cnt_ref[h, 0, i:i + 1, cs], (BF16_ROWS, LANES)).astype(BF16)
                e1_t = jnp.broadcast_to(e1_ref[h, 0, i:i + 1, cs], (BF16_ROWS, LANES)).astype(BF16)
                for k in range(n_slab):
                    js = slice(k * SUBLANES, (k + 1) * SUBLANES)
                    r2 = pltpu.bitcast(r2_ref[h, js, cs], BF16)
                    e2 = pltpu.bitcast(e2_ref[h, js, cs], BF16)
                    coef[k] = coef[k] + jnp.where(r2 < cnt_t, e2, jnp.zeros((), BF16)) * e1_t
            for k in range(n_slab):
                rows = slice(i * P_NKEYS + k * BF16_ROWS, i * P_NKEYS + (k + 1) * BF16_ROWS)
                a = a_buf[other, rows, cs]
                gl = 0.5 * a * (1.0 + lax.erf(a * _SQRT_HALF))
                g_buf[other, rows, cs] = gl.astype(BF16) * coef[k]


def _peer(dn, up_t, xn3, cnt, e1, r2, e2):
    n = xn3.shape[0]
    assert PE_ROWS == SUBLANES
    cnt, e1 = (a.reshape(P_HEADS, PE_NE, PE_ROWS, n) for a in (cnt, e1))
    rblk = pl.BlockSpec((P_HEADS, 1, PE_ROWS, PE_T), lambda t, e: (0, jnp.clip(e - 1, 0, PE_NE - 1), 0, t))
    pblk = pl.BlockSpec((P_HEADS, P_NKEYS // 2, PE_T), lambda t, e: (0, 0, t))
    return pl.pallas_call(
        _peer_kernel,
        grid=(n // PE_T, PE_NE + 2),
        in_specs=[pl.BlockSpec((PE_E, D_MODEL), lambda t, e: (jnp.minimum(e, PE_NE - 1), 0)),
                  pl.BlockSpec((D_MODEL, PE_E), lambda t, e: (0, jnp.clip(e - 2, 0, PE_NE - 1))),
                  pl.BlockSpec((PE_T, D_MODEL), lambda t, e: (t, 0)),
                  rblk, rblk, pblk, pblk],
        out_specs=pl.BlockSpec((D_MODEL, PE_T), lambda t, e: (0, t)),
        out_shape=jax.ShapeDtypeStruct((D_MODEL, n), F32),
        scratch_shapes=[pltpu.VMEM((2, PE_E, PE_T), F32), pltpu.VMEM((2, PE_E, PE_T), BF16)],
        compiler_params=_params(2, 56),
        name="peer",
    )(dn, up_t, xn3, cnt, e1, r2, e2)


FIN_TM = 512


def _final_kernel(h_ref, pt_ref, g_ref, o_ref):
    o_ref[...] = _rms(h_ref[...] + pt_ref[...].T, g_ref[...])


def _final(h2, peer_t, g_final):
    n = h2.shape[0]
    return pl.pallas_call(
        _final_kernel,
        grid=(n // FIN_TM,),
        in_specs=[pl.BlockSpec((FIN_TM, D_MODEL), lambda i: (i, 0)),
                  pl.BlockSpec((D_MODEL, FIN_TM), lambda i: (0, i)),
                  pl.BlockSpec(g_final.shape, lambda i: (0, 0))],
        out_specs=pl.BlockSpec((FIN_TM, D_MODEL), lambda i: (i, 0)),
        out_shape=jax.ShapeDtypeStruct((n, D_MODEL), F32),
        compiler_params=_params(1, 32),
        name="final",
    )(h2, peer_t, g_final)


def _pack_w_in(w):
    o = 0
    segs = {}
    for name, sz in (("mq", 256), ("mk", 256), ("mv", 512), ("mo", 512), ("mi", 4), ("mf", 4),
                     ("aq", 512), ("ak", 128), ("av", 128)):
        segs[name] = w[:, o:o + sz]
        o += sz
    k0, k1 = segs["ak"][:, :64], segs["ak"][:, 64:]
    v0, v1 = segs["av"][:, :64], segs["av"][:, 64:]
    gates = jnp.concatenate([segs["mi"], segs["mf"], jnp.zeros((w.shape[0], LANES - 8), w.dtype)], axis=1)
    packed = jnp.concatenate([segs["mq"], segs["mk"], segs["aq"] * (A_DH ** -0.5), k0, k0, k1, k1, gates], axis=1)
    w_t = jnp.concatenate([segs["mv"], segs["mo"], v0, v0, v1, v1], axis=1).T
    return packed.astype(BF16), w_t.astype(BF16)


def kernel(x, mem, g_mix, w_in, conv_w, conv_b, b_igate, b_fgate, g_mhead, sinks, w_out, g_cross, g_mem,
           w_xq, w_xk, w_xv, w_xo, g_ffn, w_pq, sub_keys1, sub_keys2, expert_down, expert_up, g_final):
    assert x.shape == (BATCH, SEQ, D_MODEL) and w_in.shape[0] == 1
    l = 0
    row = lambda v: v.reshape(1, -1).astype(F32)
    x2 = x.reshape(NTOK, D_MODEL)

    qk_pre, aq, ak, gates, mv_t, mo_t, av_t = _mix_in(x2, row(g_mix[l]), *_pack_w_in(w_in[l]))

    gate_bias = jnp.concatenate([b_igate[l], b_fgate[l], jnp.zeros((LANES - 8,), F32)]).reshape(1, LANES)
    g_mhead_b = jnp.broadcast_to(g_mhead[l].astype(F32)[:, None], (M_HEADS * M_DV, LANES))
    r3 = lambda a: a.reshape(BATCH, SEQ, a.shape[-1])
    hm = _mlstm(r3(qk_pre), mv_t, mo_t, r3(gates), conv_w[l], row(conv_b[l]), gate_bias, g_mhead_b)

    sinks_b = jnp.broadcast_to(sinks[l].astype(F32)[:, None], (A_HEADS, LANES))
    ha = _swa(r3(aq), r3(ak), av_t, sinks_b)

    kmem, vmem = _memkv(mem, row(g_mem[l]), w_xk[l].astype(BF16), w_xv[l].astype(BF16))

    h2, xn3 = _post(x2, hm.reshape(NTOK, 512), ha.reshape(NTOK, 512), w_out[l].astype(BF16), row(g_cross[l]),
                    (w_xq[l] * (X_DH ** -0.5)).astype(BF16), kmem, vmem, w_xo[l].astype(BF16), row(g_ffn[l]))

    cnt, e1, r2, e2 = _route(xn3, w_pq[l].T.astype(BF16), sub_keys1[l], sub_keys2[l])

    peer_t = _peer(expert_down[l].astype(BF16), expert_up[l].T.astype(BF16), xn3, cnt, e1, r2, e2)

    out = _final(h2, peer_t, row(g_final))
    return out.reshape(BATCH, SEQ, D_MODEL)
```

```python
import functools

import jax
import jax.numpy as jnp
from jax import lax
from jax.experimental import pallas as pl
from jax.experimental.pallas import tpu as pltpu

F32 = jnp.float32
BF16 = jnp.bfloat16

D_MODEL = 1024
BATCH = 2
SEQ = 8192
NTOK = BATCH * SEQ
MEM_LEN = 256
EPS = 1e-6

M_HEADS = 4
M_DV = 128
M_DQK = 64
M_CONV = 4
M_CHUNK = 64

A_HEADS = 8
A_KV_HEADS = 2
A_DH = 64
WINDOW = 128
A_BLOCK = 128

X_HEADS = 4
X_DH = 256

P_HEADS = 8
P_NKEYS = 128
P_EXPERTS = P_NKEYS * P_NKEYS
P_KEY_DIM = 128
P_TOPK = 16

LANES = 128
SUBLANES = 8
BF16_ROWS = 2 * SUBLANES

NEG_INF = float("-inf")

V7X_VMEM_BYTES = 64 * 1024 * 1024
MIB = 1024 * 1024


def _params(ndims, vmem_mib):
    assert vmem_mib * MIB < V7X_VMEM_BYTES
    return pltpu.CompilerParams(dimension_semantics=("arbitrary",) * ndims, vmem_limit_bytes=vmem_mib * MIB)


def _rms(xf, g):
    return xf * lax.rsqrt(jnp.mean(xf * xf, axis=-1, keepdims=True) + EPS) * g


def _dot_nt(a, b):
    return lax.dot_general(a, b, (((1,), (1,)), ((), ())), preferred_element_type=F32)


def _dot_tn(a, b):
    return lax.dot_general(a, b, (((0,), (0,)), ((), ())), preferred_element_type=F32)


def _dot(a, b):
    return jnp.dot(a, b, preferred_element_type=F32)


MIX_TM = 512


def _mix_in_kernel(x_ref, g_ref, w_ref, wt_ref, qk_ref, aq_ref, ak_ref, gt_ref, mvt_ref, mot_ref, avt_ref):
    xn = _rms(x_ref[...], g_ref[...]).astype(BF16)
    qk_ref[...] = _dot(xn, w_ref[:, 0:512])
    aq_ref[...] = _dot(xn, w_ref[:, 512:1024]).astype(BF16)
    ak_ref[...] = _dot(xn, w_ref[:, 1024:1280]).astype(BF16)
    gt_ref[...] = _dot(xn, w_ref[:, 1280:1408])
    mvt_ref[...] = _dot_nt(wt_ref[0:512, :], xn).astype(BF16)
    mot_ref[...] = _dot_nt(wt_ref[512:1024, :], xn)
    avt_ref[...] = _dot_nt(wt_ref[1024:1280, :], xn).astype(BF16)


def _mix_in(x2, g_mix, w_all, w_t):
    n = x2.shape[0]
    grid = (n // MIX_TM,)
    row = lambda c: pl.BlockSpec((MIX_TM, c), lambda i: (i, 0))
    col = lambda r: pl.BlockSpec((r, MIX_TM), lambda i: (0, i))
    full = lambda a: pl.BlockSpec(a.shape, lambda i: (0,) * a.ndim)
    return pl.pallas_call(
        _mix_in_kernel,
        grid=grid,
        in_specs=[row(D_MODEL), full(g_mix), full(w_all), full(w_t)],
        out_specs=[row(512), row(512), row(256), row(128), col(512), col(512), col(256)],
        out_shape=[
            jax.ShapeDtypeStruct((n, 512), F32),
            jax.ShapeDtypeStruct((n, 512), BF16),
            jax.ShapeDtypeStruct((n, 256), BF16),
            jax.ShapeDtypeStruct((n, 128), F32),
            jax.ShapeDtypeStruct((512, n), BF16),
            jax.ShapeDtypeStruct((512, n), F32),
            jax.ShapeDtypeStruct((256, n), BF16),
        ],
        compiler_params=_params(1, 40),
        name="mix_in",
    )(x2, g_mix, w_all, w_t)


ML_T = 128
ML_HALO = 8


def _log_sigmoid(x):
    return jnp.minimum(x, 0.0) - jnp.log1p(jnp.exp(-jnp.abs(x)))


def _mlstm_kernel(qk_ref, halo_ref, mvt0_ref, mvt1_ref, mot0_ref, mot1_ref, gt_ref, cw_ref, cb_ref, gb_ref,
                  gmh_ref, hm_ref, xcat_ref, c_ref, n_ref, m_ref):
    step = pl.program_id(0)
    mvt_refs = (mvt0_ref, mvt1_ref)
    mot_refs = (mot0_ref, mot1_ref)

    @pl.when(step == 0)
    def _():
        c_ref[...] = jnp.zeros_like(c_ref)
        n_ref[...] = jnp.zeros_like(n_ref)
        m_ref[...] = jnp.zeros_like(m_ref)

    src = lax.broadcasted_iota(jnp.int32, (ML_T, ML_T), 0)
    dst = lax.broadcasted_iota(jnp.int32, (ML_T, ML_T), 1)
    lo = dst < 64
    lo_row = lax.broadcasted_iota(jnp.int32, (1, LANES), 1) < 64
    causal = src <= dst
    tri = (src >= dst).astype(F32)
    not_first = (step > 0).astype(F32)

    for b in range(BATCH):
        mvt_ref, mot_ref = mvt_refs[b], mot_refs[b]
        xcat_ref[b, 0:ML_HALO, :] = halo_ref[b] * not_first
        xcat_ref[b, ML_HALO:ML_HALO + ML_T, :] = qk_ref[b]
        acc = jnp.broadcast_to(cb_ref[...], (ML_T, 512))
        for j in range(M_CONV):
            acc = acc + cw_ref[j:j + 1, :] * xcat_ref[b, pl.ds(ML_HALO - (M_CONV - 1) + j, ML_T), :]
        qk = acc * jax.nn.sigmoid(acc)

        gb = gt_ref[b] + gb_ref[...]
        lf = _log_sigmoid(gb)
        bc = jnp.dot(tri, lf, precision=lax.Precision.HIGHEST,
                     preferred_element_type=F32)
        b_t = bc.T

        for p in range(M_HEADS // 2):
            qp = qk[:, p * 128:(p + 1) * 128] * (M_DQK ** -0.5)
            kp = qk[:, 256 + p * 128:256 + (p + 1) * 128]
            kp_bf = kp.astype(BF16)
            nrow = n_ref[b * 2 + p][0:1, :]
            n_bf = jnp.broadcast_to(nrow, (SUBLANES, LANES)).astype(BF16)
            n_parts = []
            for hh in range(2):
                h = 2 * p + hh
                sidx = b * M_HEADS + h
                mh = lo if hh == 0 else jnp.logical_not(lo)
                qm_bf = jnp.where(mh, qp, 0.0).astype(BF16)
                s_t = _dot_nt(kp_bf, qm_bf)
                m_prev = m_ref[sidx][0:1, 0:1]
                b_row = b_t[4 + h:5 + h, :]
                col = jnp.broadcast_to(gb[:, h:h + 1] - bc[:, 4 + h:5 + h], (ML_T, ML_T))
                log_d = jnp.where(causal, b_row + col, NEG_INF)
                inter = b_row + m_prev
                m_t = jnp.maximum(inter, jnp.max(log_d, axis=0, keepdims=True))
                sp = s_t * jnp.exp(log_d - m_t)
                w_inter = jnp.exp(inter - m_t)
                qn = _dot_nt(n_bf, qm_bf)[0:1, :]
                den = jnp.sum(sp, axis=0, keepdims=True) + w_inter * qn
                v_t = mvt_ref[h * M_DV:(h + 1) * M_DV, :]
                c_h = c_ref[sidx]
                num_t = _dot(v_t, sp.astype(BF16)) + w_inter * _dot_nt(c_h.astype(BF16), qm_bf)
                h_t = num_t / jnp.maximum(jnp.abs(den), jnp.exp(-m_t))
                h_n = h_t * lax.rsqrt(jnp.mean(h_t * h_t, axis=0, keepdims=True) + EPS)
                rows_h = slice(h * M_DV, (h + 1) * M_DV)
                out_t = jax.nn.sigmoid(mot_ref[rows_h, :]) * (h_n * gmh_ref[rows_h, :])
                hm_ref[b, :, h * M_DV:(h + 1) * M_DV] = out_t.T.astype(BF16)
                m_new = m_t[:, ML_T - 1:ML_T]
                b_last = b_row[:, ML_T - 1:ML_T]
                decay = jnp.exp(b_last + m_prev - m_new)
                wk = jnp.exp(col + (b_last - m_new)) * kp
                c_ref[sidx] = decay * c_h + jnp.where(mh, _dot(v_t, wk.astype(BF16)), 0.0)
                n_parts.append(decay * nrow + jnp.sum(wk, axis=0, keepdims=True))
                m_ref[sidx] = jnp.broadcast_to(m_new, (SUBLANES, LANES))
            n_new = jnp.where(lo_row, n_parts[0], n_parts[1])
            n_ref[b * 2 + p] = jnp.broadcast_to(n_new, (SUBLANES, LANES))


def _mlstm(qk_pre, mv_t, mo_t, gates, conv_w, conv_b, gate_bias, g_mhead_b):
    nblk = SEQ // ML_T
    blk = lambda c: pl.BlockSpec((BATCH, ML_T, c), lambda i: (0, i, 0))
    full = lambda a: pl.BlockSpec(a.shape, lambda i: (0,) * a.ndim)
    halo = pl.BlockSpec((BATCH, ML_HALO, 512),
                        lambda i: (0, jnp.maximum(i * (ML_T // ML_HALO) - 1, 0), 0))
    t_blk = lambda b: pl.BlockSpec((512, ML_T), lambda i: (0, b * nblk + i))
    return pl.pallas_call(
        _mlstm_kernel,
        grid=(nblk,),
        in_specs=[blk(512), halo, t_blk(0), t_blk(1), t_blk(0), t_blk(1), blk(128),
                  full(conv_w), full(conv_b), full(gate_bias), full(g_mhead_b)],
        out_specs=blk(512),
        out_shape=jax.ShapeDtypeStruct((BATCH, SEQ, 512), BF16),
        scratch_shapes=[
            pltpu.VMEM((BATCH, ML_HALO + ML_T, 512), F32),
            pltpu.VMEM((BATCH * M_HEADS, M_DV, LANES), F32),
            pltpu.VMEM((BATCH * 2, SUBLANES, LANES), F32),
            pltpu.VMEM((BATCH * M_HEADS, SUBLANES, LANES), F32),
        ],
        compiler_params=_params(1, 32),
        name="mlstm",
    )(qk_pre, qk_pre, mv_t, mv_t, mo_t, mo_t, gates, conv_w, conv_b, gate_bias, g_mhead_b)


SWA_SUB = 4
SWA_T = SWA_SUB * A_BLOCK


def _swa_kernel(q_ref, kp_ref, kc_ref, vtp_ref, vtc_ref, sk_ref, o_ref):
    n = pl.program_id(1)
    lo = lax.broadcasted_iota(jnp.int32, (A_BLOCK, LANES), 1) < 64
    top = lax.broadcasted_iota(jnp.int32, (LANES, A_BLOCK), 0) < 64
    kj = lax.broadcasted_iota(jnp.int32, (2 * A_BLOCK, A_BLOCK), 0)
    qi = lax.broadcasted_iota(jnp.int32, (2 * A_BLOCK, A_BLOCK), 1)
    diff = qi - kj + A_BLOCK
    band = (diff >= 0) & (diff < WINDOW)
    band_first = band & ((kj >= A_BLOCK) | (n > 0))
    for sb in range(SWA_SUB):
        q_rows = slice(sb * A_BLOCK, (sb + 1) * A_BLOCK)
        mask = band_first if sb == 0 else band
        for j in range(A_KV_HEADS):
            cols = slice(j * 128, (j + 1) * 128)
            if sb == 0:
                kk = jnp.concatenate([kp_ref[0, :, cols], kc_ref[0, 0:A_BLOCK, cols]], axis=0)
                vt = jnp.concatenate([vtp_ref[cols, :], vtc_ref[cols, 0:A_BLOCK]], axis=1)
            else:
                kv_rows = slice((sb - 1) * A_BLOCK, (sb + 1) * A_BLOCK)
                kk = kc_ref[0, kv_rows, cols]
                vt = vtc_ref[cols, kv_rows]
            for p in range(2):
                c0 = j * 256 + p * 128
                qp = q_ref[0, q_rows, c0:c0 + 128].astype(F32)
                outs = []
                for hh in range(2):
                    h = j * 4 + p * 2 + hh
                    mh = lo if hh == 0 else jnp.logical_not(lo)
                    qm = jnp.where(mh, qp, 0.0).astype(BF16)
                    sc = jnp.where(mask, _dot_nt(kk, qm), NEG_INF)
                    sink = sk_ref[h:h + 1, 0:1]
                    mx = jnp.maximum(jnp.max(sc, axis=0, keepdims=True), sink)
                    pe = jnp.exp(sc - mx)
                    den = jnp.sum(pe, axis=0, keepdims=True) + jnp.exp(sink - mx)
                    outs.append(_dot(vt, pe.astype(BF16)) / den)
                o_t = jnp.where(top, outs[0], outs[1])
                o_ref[0, q_rows, c0:c0 + 128] = o_t.T.astype(BF16)


def _swa(aq, ak, av_t, sinks_b):
    per_b = SEQ // SWA_T
    cur = lambda c: pl.BlockSpec((1, SWA_T, c), lambda b, n: (b, n, 0))
    prev = lambda c: pl.BlockSpec((1, A_BLOCK, c), lambda b, n: (b, jnp.maximum(n * SWA_SUB - 1, 0), 0))
    vt_cur = pl.BlockSpec((256, SWA_T), lambda b, n: (0, b * per_b + n))
    vt_prev = pl.BlockSpec((256, A_BLOCK), lambda b, n: (0, b * per_b * SWA_SUB + jnp.maximum(n * SWA_SUB - 1, 0)))
    return pl.pallas_call(
        _swa_kernel,
        grid=(BATCH, per_b),
        in_specs=[cur(512), prev(256), cur(256), vt_prev, vt_cur,
                  pl.BlockSpec(sinks_b.shape, lambda b, n: (0, 0))],
        out_specs=cur(512),
        out_shape=jax.ShapeDtypeStruct((BATCH, SEQ, 512), BF16),
        compiler_params=_params(2, 32),
        name="swa",
    )(aq, ak, ak, av_t, av_t, sinks_b)


def _memkv_kernel(mem_ref, g_ref, wk_ref, wv_ref, k_ref, v_ref):
    mn = _rms(mem_ref[0], g_ref[...]).astype(BF16)
    k_ref[0] = _dot(mn, wk_ref[...]).astype(BF16)
    v_ref[0] = _dot(mn, wv_ref[...]).astype(BF16)


def _memkv(mem, g_mem, w_xk, w_xv):
    blk = pl.BlockSpec((1, MEM_LEN, D_MODEL), lambda b: (b, 0, 0))
    full = lambda a: pl.BlockSpec(a.shape, lambda b: (0,) * a.ndim)
    return pl.pallas_call(
        _memkv_kernel,
        grid=(BATCH,),
        in_specs=[blk, full(g_mem), full(w_xk), full(w_xv)],
        out_specs=[blk, blk],
        out_shape=[jax.ShapeDtypeStruct((BATCH, MEM_LEN, D_MODEL), BF16)] * 2,
        compiler_params=_params(1, 32),
        name="memkv",
    )(mem, g_mem, w_xk, w_xv)


POST_TM = 512


def _post_kernel(x_ref, hm_ref, ha_ref, wo_ref, gc_ref, wq_ref, km_ref, vm_ref, wxo_ref, gf_ref,
                 h2_ref, xn3_ref):
    h1 = x_ref[...] + _dot(hm_ref[...], wo_ref[0:512, :]) + _dot(ha_ref[...], wo_ref[512:1024, :])
    xn2 = _rms(h1, gc_ref[...]).astype(BF16)
    q = _dot(xn2, wq_ref[...]).astype(BF16)
    heads = []
    for hd in range(X_HEADS):
        cols = slice(hd * X_DH, (hd + 1) * X_DH)
        sc = _dot_nt(q[:, cols], km_ref[0, :, cols])
        mx = jnp.max(sc, axis=-1, keepdims=True)
        pe = jnp.exp(sc - mx)
        den = jnp.sum(pe, axis=-1, keepdims=True)
        heads.append((_dot(pe.astype(BF16), vm_ref[0, :, cols]) / den).astype(BF16))
    h2 = h1 + _dot(jnp.concatenate(heads, axis=1), wxo_ref[...])
    h2_ref[...] = h2
    xn3_ref[...] = _rms(h2, gf_ref[...]).astype(BF16)


def _post(x2, hm2, ha2, w_out, g_cross, w_xq, kmem, vmem, w_xo, g_ffn):
    n = x2.shape[0]
    per_batch = SEQ // POST_TM
    row = lambda c: pl.BlockSpec((POST_TM, c), lambda i: (i, 0))
    full = lambda a: pl.BlockSpec(a.shape, lambda i: (0,) * a.ndim)
    memblk = pl.BlockSpec((1, MEM_LEN, D_MODEL), lambda i: (i // per_batch, 0, 0))
    return pl.pallas_call(
        _post_kernel,
        grid=(n // POST_TM,),
        in_specs=[row(D_MODEL), row(512), row(512), full(w_out), full(g_cross), full(w_xq),
                  memblk, memblk, full(w_xo), full(g_ffn)],
        out_specs=[row(D_MODEL), row(D_MODEL)],
        out_shape=[jax.ShapeDtypeStruct((n, D_MODEL), F32), jax.ShapeDtypeStruct((n, D_MODEL), BF16)],
        compiler_params=_params(1, 48),
        name="post",
    )(x2, hm2, ha2, w_out, g_cross, w_xq, kmem, vmem, w_xo, g_ffn)


RT_T = 128


def _ce(vals, i, j):
    a, b = vals[i], vals[j]
    vals[i] = jnp.maximum(a, b)
    vals[j] = jnp.minimum(a, b)


def _bitonic_sort_desc(vals):
    n = len(vals)
    k = 2
    while k <= n:
        j = k // 2
        while j >= 1:
            for i in range(n):
                l = i ^ j
                if l > i:
                    if (i & k) == 0:
                        _ce(vals, i, l)
                    else:
                        _ce(vals, l, i)
            j //= 2
        k *= 2


def _bitonic_merge_desc(vals):
    n = len(vals)
    j = n // 2
    while j >= 1:
        for i in range(n):
            l = i ^ j
            if l > i:
                _ce(vals, i, l)
        j //= 2


def _merge_top16(xs, ys):
    m = [jnp.maximum(xs[a], ys[P_TOPK - 1 - a]) for a in range(P_TOPK)]
    _bitonic_merge_desc(m)
    return m


def _top16_over_keys(vals):
    vals = list(vals)
    _bitonic_sort_desc(vals)
    for sh in (4, 2, 1):
        rolled = [pltpu.roll(v, sh, axis=0) for v in vals]
        vals = _merge_top16(vals, rolled)
    return vals


def _split_bf16(x):
    hi = x.astype(BF16)
    return hi, (x - hi.astype(F32)).astype(BF16)


def _route_kernel(xn_ref, wq_ref, k1_ref, k2_ref, cnt_ref, e1_ref, r2_ref, e2_ref):
    q_t = _dot_nt(wq_ref[...], xn_ref[...])
    k_parts = [_split_bf16(k1_ref[...]), _split_bf16(k2_ref[...])]
    s1_all, s2_all = [], []
    for h in range(P_HEADS):
        for half, dst in enumerate((s1_all, s2_all)):
            r0 = (2 * h + half) * P_KEY_DIM
            q_hi, q_lo = _split_bf16(q_t[r0:r0 + P_KEY_DIM, :])
            k_hi, k_lo = k_parts[half]
            main = _dot(k_hi, jnp.concatenate([q_hi, q_lo], axis=1))
            dst.append(main[:, 0:RT_T] + main[:, RT_T:2 * RT_T] + _dot(k_lo, q_hi))

    probs = [s for pair in zip(s1_all, s2_all) for s in pair]
    wide = [jnp.concatenate([s[kb * SUBLANES:(kb + 1) * SUBLANES, :] for s in probs], axis=1)
            for kb in range(P_NKEYS // SUBLANES)]
    tops = _top16_over_keys(wide)
    sub = lax.broadcasted_iota(jnp.int32, (SUBLANES, RT_T), 0)
    v1 = [None] * P_TOPK
    v2 = [None] * P_TOPK
    for a in range(P_TOPK):
        for h in range(P_HEADS):
            t1 = tops[a][:, (2 * h) * RT_T:(2 * h + 1) * RT_T]
            t2 = tops[a][:, (2 * h + 1) * RT_T:(2 * h + 2) * RT_T]
            v1[a] = t1 if h == 0 else jnp.where(sub == h, t1, v1[a])
            v2[a] = t2 if h == 0 else jnp.where(sub == h, t2, v2[a])
    pairs = [(a, b) for a in range(P_TOPK) for b in range(P_TOPK // (a + 1))]
    cand_of = {ab: v1[ab[0]] + v2[ab[1]] for ab in pairs}
    pad = jnp.full((SUBLANES, RT_T), NEG_INF, F32)
    cands = [cand_of[ab] for ab in pairs] + [pad] * (64 - len(pairs))
    groups = []
    for gi in range(4):
        grp = cands[gi::4]
        _bitonic_sort_desc(grp)
        groups.append(grp)
    top = _merge_top16(_merge_top16(groups[0], groups[1]), _merge_top16(groups[2], groups[3]))
    tau = top[P_TOPK - 1]
    z = jnp.ones_like(tau)
    for kk in range(1, P_TOPK):
        z = z + jnp.exp(top[kk] - top[0])
    zinv = 1.0 / z
    cnt_rank = []
    for a in range(P_TOPK):
        c_a = jnp.zeros((SUBLANES, RT_T), F32)
        for b in range(P_TOPK // (a + 1)):
            c_a = jnp.where(cand_of[(a, b)] >= tau, float(b + 1), c_a)
        cnt_rank.append(c_a)
    for h in range(P_HEADS):
        s1, s2 = s1_all[h], s2_all[h]
        cnt_rows = jnp.zeros((P_NKEYS, RT_T), F32)
        for a in range(P_TOPK):
            cnt_rows = jnp.where(s1 == v1[a][h:h + 1, :], cnt_rank[a][h:h + 1, :], cnt_rows)
        rank2 = jnp.zeros((P_NKEYS, RT_T), F32)
        for b in range(P_TOPK):
            rank2 = jnp.where(s2 < v2[b][h:h + 1, :], float(b + 1), rank2)
        cnt_ref[h] = cnt_rows
        e1_ref[h] = jnp.exp(s1 - v1[0][h:h + 1, :]) * zinv[h:h + 1, :]
        r2_ref[h] = pltpu.bitcast(rank2.astype(BF16), jnp.uint32)
        e2_ref[h] = pltpu.bitcast(jnp.exp(s2 - v2[0][h:h + 1, :]).astype(BF16), jnp.uint32)


def _route(xn3, w_pq_t, keys1, keys2):
    n = xn3.shape[0]
    full = lambda a: pl.BlockSpec(a.shape, lambda i: (0,) * a.ndim)
    oblk = lambda rows: pl.BlockSpec((P_HEADS, rows, RT_T), lambda i: (0, 0, i))
    f32_rows = jax.ShapeDtypeStruct((P_HEADS, P_NKEYS, n), F32)
    packed_rows = jax.ShapeDtypeStruct((P_HEADS, P_NKEYS // 2, n), jnp.uint32)
    return pl.pallas_call(
        _route_kernel,
        grid=(n // RT_T,),
        in_specs=[pl.BlockSpec((RT_T, D_MODEL), lambda i: (i, 0)), full(w_pq_t), full(keys1), full(keys2)],
        out_specs=[oblk(P_NKEYS), oblk(P_NKEYS), oblk(P_NKEYS // 2), oblk(P_NKEYS // 2)],
        out_shape=[f32_rows, f32_rows, packed_rows, packed_rows],
        compiler_params=_params(1, 32),
        name="route",
    )(xn3, w_pq_t, keys1, keys2)


PE_T = 1024
PE_E = 1024
_SQRT_HALF = 0.7071067811865476


PE_NE = P_EXPERTS // PE_E
PE_ROWS = PE_E // P_NKEYS
PE_ROW_GROUP = 1


def _peer_kernel(dn_ref, up_ref, xn_ref, cnt_ref, e1_ref, r2_ref, e2_ref, o_ref, a_buf, g_buf):
    s = pl.program_id(0)
    slot = lax.rem(s, 2)
    other = 1 - slot

    @pl.when(s == 0)
    def _():
        a_buf[1] = jnp.zeros((PE_E, PE_T), F32)
        g_buf[0] = jnp.zeros((PE_E, PE_T), BF16)

    @pl.when((s == 0) | ((s >= 2) & (lax.rem(s - 2, PE_NE) == 0)))
    def _():
        o_ref[...] = jnp.zeros_like(o_ref)

    a_buf[slot] = _dot_nt(dn_ref[...], xn_ref[...])
    o_ref[...] += _dot(up_ref[...], g_buf[slot])

    n_slab = P_NKEYS // BF16_ROWS
    zero = jnp.zeros((), BF16)
    for c in range(PE_T // LANES):
        cs = slice(c * LANES, (c + 1) * LANES)
        for i0 in range(0, PE_ROWS, PE_ROW_GROUP):
            group = range(i0, i0 + PE_ROW_GROUP)
            coef = {i: [jnp.zeros((BF16_ROWS, LANES), BF16)] * n_slab for i in group}
            for h in range(P_HEADS):
                cnt_t = {i: jnp.broadcast_to(cnt_ref[h, 0, i:i + 1, cs], (BF16_ROWS, LANES)).astype(BF16)
                         for i in group}
                e1_t = {i: jnp.broadcast_to(e1_ref[h, 0, i:i + 1, cs], (BF16_ROWS, LANES)).astype(BF16)
                        for i in group}
                for k in range(n_slab):
                    js = slice(k * SUBLANES, (k + 1) * SUBLANES)
                    r2 = pltpu.bitcast(r2_ref[h, js, cs], BF16)
                    e2 = pltpu.bitcast(e2_ref[h, js, cs], BF16)
                    for i in group:
                        coef[i][k] = coef[i][k] + jnp.where(r2 < cnt_t[i], e2, zero) * e1_t[i]
            for i in group:
                for k in range(n_slab):
                    rows = slice(i * P_NKEYS + k * BF16_ROWS, i * P_NKEYS + (k + 1) * BF16_ROWS)
                    a = a_buf[other, rows, cs]
                    gl = 0.5 * a * (1.0 + lax.erf(a * _SQRT_HALF))
                    g_buf[other, rows, cs] = gl.astype(BF16) * coef[i][k]


def _peer(dn, up_t, xn3, cnt, e1, r2, e2):
    n = xn3.shape[0]
    assert PE_ROWS == SUBLANES
    n_pairs = (n // PE_T) * PE_NE
    pair = lambda s, lag: jnp.clip(s - lag, 0, n_pairs - 1)
    tok = lambda s, lag: pair(s, lag) // PE_NE
    tile = lambda s, lag: pair(s, lag) % PE_NE
    cnt, e1 = (a.reshape(P_HEADS, PE_NE, PE_ROWS, n) for a in (cnt, e1))
    rblk = pl.BlockSpec((P_HEADS, 1, PE_ROWS, PE_T), lambda s: (0, tile(s, 1), 0, tok(s, 1)))
    pblk = pl.BlockSpec((P_HEADS, P_NKEYS // 2, PE_T), lambda s: (0, 0, tok(s, 1)))
    return pl.pallas_call(
        _peer_kernel,
        grid=(n_pairs + 2,),
        in_specs=[pl.BlockSpec((PE_E, D_MODEL), lambda s: (tile(s, 0), 0)),
                  pl.BlockSpec((D_MODEL, PE_E), lambda s: (0, tile(s, 2))),
                  pl.BlockSpec((PE_T, D_MODEL), lambda s: (tok(s, 0), 0)),
                  rblk, rblk, pblk, pblk],
        out_specs=pl.BlockSpec((D_MODEL, PE_T), lambda s: (0, tok(s, 2))),
        out_shape=jax.ShapeDtypeStruct((D_MODEL, n), F32),
        scratch_shapes=[pltpu.VMEM((2, PE_E, PE_T), F32), pltpu.VMEM((2, PE_E, PE_T), BF16)],
        compiler_params=_params(1, 56),
        name="peer",
    )(dn, up_t, xn3, cnt, e1, r2, e2)


FIN_TM = 512


def _final_kernel(h_ref, pt_ref, g_ref, o_ref):
    o_ref[...] = _rms(h_ref[...] + pt_ref[...].T, g_ref[...])


def _final(h2, peer_t, g_final):
    n = h2.shape[0]
    return pl.pallas_call(
        _final_kernel,
        grid=(n // FIN_TM,),
        in_specs=[pl.BlockSpec((FIN_TM, D_MODEL), lambda i: (i, 0)),
                  pl.BlockSpec((D_MODEL, FIN_TM), lambda i: (0, i)),
                  pl.BlockSpec(g_final.shape, lambda i: (0, 0))],
        out_specs=pl.BlockSpec((FIN_TM, D_MODEL), lambda i: (i, 0)),
        out_shape=jax.ShapeDtypeStruct((n, D_MODEL), F32),
        compiler_params=_params(1, 32),
        name="final",
    )(h2, peer_t, g_final)


def _pack_w_in(w):
    o = 0
    segs = {}
    for name, sz in (("mq", 256), ("mk", 256), ("mv", 512), ("mo", 512), ("mi", 4), ("mf", 4),
                     ("aq", 512), ("ak", 128), ("av", 128)):
        segs[name] = w[:, o:o + sz]
        o += sz
    k0, k1 = segs["ak"][:, :64], segs["ak"][:, 64:]
    v0, v1 = segs["av"][:, :64], segs["av"][:, 64:]
    gates = jnp.concatenate([segs["mi"], segs["mf"], jnp.zeros((w.shape[0], LANES - 8), w.dtype)], axis=1)
    packed = jnp.concatenate([segs["mq"], segs["mk"], segs["aq"] * (A_DH ** -0.5), k0, k0, k1, k1, gates], axis=1)
    w_t = jnp.concatenate([segs["mv"], segs["mo"], v0, v0, v1, v1], axis=1).T
    return packed.astype(BF16), w_t.astype(BF16)


def kernel(x, mem, g_mix, w_in, conv_w, conv_b, b_igate, b_fgate, g_mhead, sinks, w_out, g_cross, g_mem,
           w_xq, w_xk, w_xv, w_xo, g_ffn, w_pq, sub_keys1, sub_keys2, expert_down, expert_up, g_final):
    assert x.shape == (BATCH, SEQ, D_MODEL) and w_in.shape[0] == 1
    l = 0
    row = lambda v: v.reshape(1, -1).astype(F32)
    x2 = x.reshape(NTOK, D_MODEL)

    qk_pre, aq, ak, gates, mv_t, mo_t, av_t = _mix_in(x2, row(g_mix[l]), *_pack_w_in(w_in[l]))

    gate_bias = jnp.concatenate([b_igate[l], b_fgate[l], jnp.zeros((LANES - 8,), F32)]).reshape(1, LANES)
    g_mhead_b = jnp.broadcast_to(g_mhead[l].astype(F32)[:, None], (M_HEADS * M_DV, LANES))
    r3 = lambda a: a.reshape(BATCH, SEQ, a.shape[-1])
    hm = _mlstm(r3(qk_pre), mv_t, mo_t, r3(gates), conv_w[l], row(conv_b[l]), gate_bias, g_mhead_b)

    sinks_b = jnp.broadcast_to(sinks[l].astype(F32)[:, None], (A_HEADS, LANES))
    ha = _swa(r3(aq), r3(ak), av_t, sinks_b)

    kmem, vmem = _memkv(mem, row(g_mem[l]), w_xk[l].astype(BF16), w_xv[l].astype(BF16))

    h2, xn3 = _post(x2, hm.reshape(NTOK, 512), ha.reshape(NTOK, 512), w_out[l].astype(BF16), row(g_cross[l]),
                    (w_xq[l] * (X_DH ** -0.5)).astype(BF16), kmem, vmem, w_xo[l].astype(BF16), row(g_ffn[l]))

    cnt, e1, r2, e2 = _route(xn3, w_pq[l].T.astype(BF16), sub_keys1[l], sub_keys2[l])

    peer_t = _peer(expert_down[l].astype(BF16), expert_up[l].T.astype(BF16), xn3, cnt, e1, r2, e2)

    out = _final(h2, peer_t, row(g_final))
    return out.reshape(BATCH, SEQ, D_MODEL)
```

```python
import functools

import jax
import jax.numpy as jnp
from jax import lax
from jax.experimental import pallas as pl
from jax.experimental.pallas import tpu as pltpu

F32 = jnp.float32
BF16 = jnp.bfloat16

D_MODEL = 1024
BATCH = 2
SEQ = 8192
NTOK = BATCH * SEQ
MEM_LEN = 256
EPS = 1e-6

M_HEADS = 4
M_DV = 128
M_DQK = 64
M_CONV = 4
M_CHUNK = 64

A_HEADS = 8
A_KV_HEADS = 2
A_DH = 64
WINDOW = 128
A_BLOCK = 128

X_HEADS = 4
X_DH = 256

P_HEADS = 8
P_NKEYS = 128
P_EXPERTS = P_NKEYS * P_NKEYS
P_KEY_DIM = 128
P_TOPK = 16

LANES = 128
SUBLANES = 8
BF16_ROWS = 2 * SUBLANES

NEG_INF = float("-inf")

V7X_VMEM_BYTES = 64 * 1024 * 1024
MIB = 1024 * 1024


def _params(ndims, vmem_mib, flags=None):
    assert vmem_mib * MIB < V7X_VMEM_BYTES
    return pltpu.CompilerParams(dimension_semantics=("arbitrary",) * ndims, vmem_limit_bytes=vmem_mib * MIB,
                                flags=flags)


def _rms(xf, g):
    return xf * lax.rsqrt(jnp.mean(xf * xf, axis=-1, keepdims=True) + EPS) * g


def _dot_nt(a, b):
    return lax.dot_general(a, b, (((1,), (1,)), ((), ())), preferred_element_type=F32)


def _dot_tn(a, b):
    return lax.dot_general(a, b, (((0,), (0,)), ((), ())), preferred_element_type=F32)


def _dot(a, b):
    return jnp.dot(a, b, preferred_element_type=F32)


MIX_TM = 512


def _mix_in_kernel(x_ref, g_ref, w_ref, wt_ref, qk_ref, aq_ref, ak_ref, gt_ref, mvt_ref, mot_ref, avt_ref):
    xn = _rms(x_ref[...], g_ref[...]).astype(BF16)
    qk_ref[...] = _dot(xn, w_ref[:, 0:512])
    aq_ref[...] = _dot(xn, w_ref[:, 512:1024]).astype(BF16)
    ak_ref[...] = _dot(xn, w_ref[:, 1024:1280]).astype(BF16)
    gt_ref[...] = _dot(xn, w_ref[:, 1280:1408])
    mvt_ref[...] = _dot_nt(wt_ref[0:512, :], xn).astype(BF16)
    mot_ref[...] = _dot_nt(wt_ref[512:1024, :], xn)
    avt_ref[...] = _dot_nt(wt_ref[1024:1280, :], xn).astype(BF16)


def _mix_in(x2, g_mix, w_all, w_t):
    n = x2.shape[0]
    grid = (n // MIX_TM,)
    row = lambda c: pl.BlockSpec((MIX_TM, c), lambda i: (i, 0))
    col = lambda r: pl.BlockSpec((r, MIX_TM), lambda i: (0, i))
    full = lambda a: pl.BlockSpec(a.shape, lambda i: (0,) * a.ndim)
    return pl.pallas_call(
        _mix_in_kernel,
        grid=grid,
        in_specs=[row(D_MODEL), full(g_mix), full(w_all), full(w_t)],
        out_specs=[row(512), row(512), row(256), row(128), col(512), col(512), col(256)],
        out_shape=[
            jax.ShapeDtypeStruct((n, 512), F32),
            jax.ShapeDtypeStruct((n, 512), BF16),
            jax.ShapeDtypeStruct((n, 256), BF16),
            jax.ShapeDtypeStruct((n, 128), F32),
            jax.ShapeDtypeStruct((512, n), BF16),
            jax.ShapeDtypeStruct((512, n), F32),
            jax.ShapeDtypeStruct((256, n), BF16),
        ],
        compiler_params=_params(1, 40),
        name="mix_in",
    )(x2, g_mix, w_all, w_t)


ML_T = 128
ML_HALO = 8


def _log_sigmoid(x):
    return jnp.minimum(x, 0.0) - jnp.log1p(jnp.exp(-jnp.abs(x)))


def _mlstm_kernel(qk_ref, halo_ref, mvt0_ref, mvt1_ref, mot0_ref, mot1_ref, gt_ref, cw_ref, cb_ref, gb_ref,
                  gmh_ref, hm_ref, xcat_ref, c_ref, n_ref, m_ref):
    step = pl.program_id(0)
    mvt_refs = (mvt0_ref, mvt1_ref)
    mot_refs = (mot0_ref, mot1_ref)

    @pl.when(step == 0)
    def _():
        c_ref[...] = jnp.zeros_like(c_ref)
        n_ref[...] = jnp.zeros_like(n_ref)
        m_ref[...] = jnp.zeros_like(m_ref)

    src = lax.broadcasted_iota(jnp.int32, (ML_T, ML_T), 0)
    dst = lax.broadcasted_iota(jnp.int32, (ML_T, ML_T), 1)
    lo = dst < 64
    lo_row = lax.broadcasted_iota(jnp.int32, (1, LANES), 1) < 64
    causal = src <= dst
    tri = (src >= dst).astype(F32)
    not_first = (step > 0).astype(F32)

    for b in range(BATCH):
        mvt_ref, mot_ref = mvt_refs[b], mot_refs[b]
        xcat_ref[b, 0:ML_HALO, :] = halo_ref[b] * not_first
        xcat_ref[b, ML_HALO:ML_HALO + ML_T, :] = qk_ref[b]
        acc = jnp.broadcast_to(cb_ref[...], (ML_T, 512))
        for j in range(M_CONV):
            acc = acc + cw_ref[j:j + 1, :] * xcat_ref[b, pl.ds(ML_HALO - (M_CONV - 1) + j, ML_T), :]
        qk = acc * jax.nn.sigmoid(acc)

        gb = gt_ref[b] + gb_ref[...]
        lf = _log_sigmoid(gb)
        bc = jnp.dot(tri, lf, precision=lax.Precision.HIGHEST,
                     preferred_element_type=F32)
        b_t = bc.T

        for p in range(M_HEADS // 2):
            qp = qk[:, p * 128:(p + 1) * 128] * (M_DQK ** -0.5)
            kp = qk[:, 256 + p * 128:256 + (p + 1) * 128]
            kp_bf = kp.astype(BF16)
            nrow = n_ref[b * 2 + p][0:1, :]
            n_bf = jnp.broadcast_to(nrow, (SUBLANES, LANES)).astype(BF16)
            n_parts = []
            for hh in range(2):
                h = 2 * p + hh
                sidx = b * M_HEADS + h
                mh = lo if hh == 0 else jnp.logical_not(lo)
                qm_bf = jnp.where(mh, qp, 0.0).astype(BF16)
                s_t = _dot_nt(kp_bf, qm_bf)
                m_prev = m_ref[sidx][0:1, 0:1]
                b_row = b_t[4 + h:5 + h, :]
                col = jnp.broadcast_to(gb[:, h:h + 1] - bc[:, 4 + h:5 + h], (ML_T, ML_T))
                log_d = jnp.where(causal, b_row + col, NEG_INF)
                inter = b_row + m_prev
                m_t = jnp.maximum(inter, jnp.max(log_d, axis=0, keepdims=True))
                sp = s_t * jnp.exp(log_d - m_t)
                w_inter = jnp.exp(inter - m_t)
                qn = _dot_nt(n_bf, qm_bf)[0:1, :]
                den = jnp.sum(sp, axis=0, keepdims=True) + w_inter * qn
                v_t = mvt_ref[h * M_DV:(h + 1) * M_DV, :]
                c_h = c_ref[sidx]
                num_t = _dot(v_t, sp.astype(BF16)) + w_inter * _dot_nt(c_h.astype(BF16), qm_bf)
                h_t = num_t / jnp.maximum(jnp.abs(den), jnp.exp(-m_t))
                h_n = h_t * lax.rsqrt(jnp.mean(h_t * h_t, axis=0, keepdims=True) + EPS)
                rows_h = slice(h * M_DV, (h + 1) * M_DV)
                out_t = jax.nn.sigmoid(mot_ref[rows_h, :]) * (h_n * gmh_ref[rows_h, :])
                hm_ref[b, :, h * M_DV:(h + 1) * M_DV] = out_t.T.astype(BF16)
                m_new = m_t[:, ML_T - 1:ML_T]
                b_last = b_row[:, ML_T - 1:ML_T]
                decay = jnp.exp(b_last + m_prev - m_new)
                wk = jnp.exp(col + (b_last - m_new)) * kp
                c_ref[sidx] = decay * c_h + jnp.where(mh, _dot(v_t, wk.astype(BF16)), 0.0)
                n_parts.append(decay * nrow + jnp.sum(wk, axis=0, keepdims=True))
                m_ref[sidx] = jnp.broadcast_to(m_new, (SUBLANES, LANES))
            n_new = jnp.where(lo_row, n_parts[0], n_parts[1])
            n_ref[b * 2 + p] = jnp.broadcast_to(n_new, (SUBLANES, LANES))


def _mlstm(qk_pre, mv_t, mo_t, gates, conv_w, conv_b, gate_bias, g_mhead_b):
    nblk = SEQ // ML_T
    blk = lambda c: pl.BlockSpec((BATCH, ML_T, c), lambda i: (0, i, 0))
    full = lambda a: pl.BlockSpec(a.shape, lambda i: (0,) * a.ndim)
    halo = pl.BlockSpec((BATCH, ML_HALO, 512),
                        lambda i: (0, jnp.maximum(i * (ML_T // ML_HALO) - 1, 0), 0))
    t_blk = lambda b: pl.BlockSpec((512, ML_T), lambda i: (0, b * nblk + i))
    return pl.pallas_call(
        _mlstm_kernel,
        grid=(nblk,),
        in_specs=[blk(512), halo, t_blk(0), t_blk(1), t_blk(0), t_blk(1), blk(128),
                  full(conv_w), full(conv_b), full(gate_bias), full(g_mhead_b)],
        out_specs=blk(512),
        out_shape=jax.ShapeDtypeStruct((BATCH, SEQ, 512), BF16),
        scratch_shapes=[
            pltpu.VMEM((BATCH, ML_HALO + ML_T, 512), F32),
            pltpu.VMEM((BATCH * M_HEADS, M_DV, LANES), F32),
            pltpu.VMEM((BATCH * 2, SUBLANES, LANES), F32),
            pltpu.VMEM((BATCH * M_HEADS, SUBLANES, LANES), F32),
        ],
        compiler_params=_params(1, 32),
        name="mlstm",
    )(qk_pre, qk_pre, mv_t, mv_t, mo_t, mo_t, gates, conv_w, conv_b, gate_bias, g_mhead_b)


SWA_SUB = 4
SWA_T = SWA_SUB * A_BLOCK


def _swa_kernel(q_ref, kp_ref, kc_ref, vtp_ref, vtc_ref, sk_ref, o_ref):
    n = pl.program_id(1)
    lo = lax.broadcasted_iota(jnp.int32, (A_BLOCK, LANES), 1) < 64
    top = lax.broadcasted_iota(jnp.int32, (LANES, A_BLOCK), 0) < 64
    kj = lax.broadcasted_iota(jnp.int32, (2 * A_BLOCK, A_BLOCK), 0)
    qi = lax.broadcasted_iota(jnp.int32, (2 * A_BLOCK, A_BLOCK), 1)
    diff = qi - kj + A_BLOCK
    band = (diff >= 0) & (diff < WINDOW)
    band_first = band & ((kj >= A_BLOCK) | (n > 0))
    for sb in range(SWA_SUB):
        q_rows = slice(sb * A_BLOCK, (sb + 1) * A_BLOCK)
        mask = band_first if sb == 0 else band
        for j in range(A_KV_HEADS):
            cols = slice(j * 128, (j + 1) * 128)
            if sb == 0:
                kk = jnp.concatenate([kp_ref[0, :, cols], kc_ref[0, 0:A_BLOCK, cols]], axis=0)
                vt = jnp.concatenate([vtp_ref[cols, :], vtc_ref[cols, 0:A_BLOCK]], axis=1)
            else:
                kv_rows = slice((sb - 1) * A_BLOCK, (sb + 1) * A_BLOCK)
                kk = kc_ref[0, kv_rows, cols]
                vt = vtc_ref[cols, kv_rows]
            for p in range(2):
                c0 = j * 256 + p * 128
                qp = q_ref[0, q_rows, c0:c0 + 128].astype(F32)
                outs = []
                for hh in range(2):
                    h = j * 4 + p * 2 + hh
                    mh = lo if hh == 0 else jnp.logical_not(lo)
                    qm = jnp.where(mh, qp, 0.0).astype(BF16)
                    sc = jnp.where(mask, _dot_nt(kk, qm), NEG_INF)
                    sink = sk_ref[h:h + 1, 0:1]
                    mx = jnp.maximum(jnp.max(sc, axis=0, keepdims=True), sink)
                    pe = jnp.exp(sc - mx)
                    den = jnp.sum(pe, axis=0, keepdims=True) + jnp.exp(sink - mx)
                    outs.append(_dot(vt, pe.astype(BF16)) / den)
                o_t = jnp.where(top, outs[0], outs[1])
                o_ref[0, q_rows, c0:c0 + 128] = o_t.T.astype(BF16)


def _swa(aq, ak, av_t, sinks_b):
    per_b = SEQ // SWA_T
    cur = lambda c: pl.BlockSpec((1, SWA_T, c), lambda b, n: (b, n, 0))
    prev = lambda c: pl.BlockSpec((1, A_BLOCK, c), lambda b, n: (b, jnp.maximum(n * SWA_SUB - 1, 0), 0))
    vt_cur = pl.BlockSpec((256, SWA_T), lambda b, n: (0, b * per_b + n))
    vt_prev = pl.BlockSpec((256, A_BLOCK), lambda b, n: (0, b * per_b * SWA_SUB + jnp.maximum(n * SWA_SUB - 1, 0)))
    return pl.pallas_call(
        _swa_kernel,
        grid=(BATCH, per_b),
        in_specs=[cur(512), prev(256), cur(256), vt_prev, vt_cur,
                  pl.BlockSpec(sinks_b.shape, lambda b, n: (0, 0))],
        out_specs=cur(512),
        out_shape=jax.ShapeDtypeStruct((BATCH, SEQ, 512), BF16),
        compiler_params=_params(2, 32),
        name="swa",
    )(aq, ak, ak, av_t, av_t, sinks_b)


def _memkv_kernel(mem_ref, g_ref, wk_ref, wv_ref, k_ref, v_ref):
    mn = _rms(mem_ref[0], g_ref[...]).astype(BF16)
    k_ref[0] = _dot(mn, wk_ref[...]).astype(BF16)
    v_ref[0] = _dot(mn, wv_ref[...]).astype(BF16)


def _memkv(mem, g_mem, w_xk, w_xv):
    blk = pl.BlockSpec((1, MEM_LEN, D_MODEL), lambda b: (b, 0, 0))
    full = lambda a: pl.BlockSpec(a.shape, lambda b: (0,) * a.ndim)
    return pl.pallas_call(
        _memkv_kernel,
        grid=(BATCH,),
        in_specs=[blk, full(g_mem), full(w_xk), full(w_xv)],
        out_specs=[blk, blk],
        out_shape=[jax.ShapeDtypeStruct((BATCH, MEM_LEN, D_MODEL), BF16)] * 2,
        compiler_params=_params(1, 32),
        name="memkv",
    )(mem, g_mem, w_xk, w_xv)


POST_TM = 512


def _post_kernel(x_ref, hm_ref, ha_ref, wo_ref, gc_ref, wq_ref, km_ref, vm_ref, wxo_ref, gf_ref,
                 h2_ref, xn3_ref):
    h1 = x_ref[...] + _dot(hm_ref[...], wo_ref[0:512, :]) + _dot(ha_ref[...], wo_ref[512:1024, :])
    xn2 = _rms(h1, gc_ref[...]).astype(BF16)
    q = _dot(xn2, wq_ref[...]).astype(BF16)
    heads = []
    for hd in range(X_HEADS):
        cols = slice(hd * X_DH, (hd + 1) * X_DH)
        sc = _dot_nt(q[:, cols], km_ref[0, :, cols])
        mx = jnp.max(sc, axis=-1, keepdims=True)
        pe = jnp.exp(sc - mx)
        den = jnp.sum(pe, axis=-1, keepdims=True)
        heads.append((_dot(pe.astype(BF16), vm_ref[0, :, cols]) / den).astype(BF16))
    h2 = h1 + _dot(jnp.concatenate(heads, axis=1), wxo_ref[...])
    h2_ref[...] = h2
    xn3_ref[...] = _rms(h2, gf_ref[...]).astype(BF16)


def _post(x2, hm2, ha2, w_out, g_cross, w_xq, kmem, vmem, w_xo, g_ffn):
    n = x2.shape[0]
    per_batch = SEQ // POST_TM
    row = lambda c: pl.BlockSpec((POST_TM, c), lambda i: (i, 0))
    full = lambda a: pl.BlockSpec(a.shape, lambda i: (0,) * a.ndim)
    memblk = pl.BlockSpec((1, MEM_LEN, D_MODEL), lambda i: (i // per_batch, 0, 0))
    return pl.pallas_call(
        _post_kernel,
        grid=(n // POST_TM,),
        in_specs=[row(D_MODEL), row(512), row(512), full(w_out), full(g_cross), full(w_xq),
                  memblk, memblk, full(w_xo), full(g_ffn)],
        out_specs=[row(D_MODEL), row(D_MODEL)],
        out_shape=[jax.ShapeDtypeStruct((n, D_MODEL), F32), jax.ShapeDtypeStruct((n, D_MODEL), BF16)],
        compiler_params=_params(1, 48),
        name="post",
    )(x2, hm2, ha2, w_out, g_cross, w_xq, kmem, vmem, w_xo, g_ffn)


RT_T = 128


def _ce(vals, i, j):
    a, b = vals[i], vals[j]
    vals[i] = jnp.maximum(a, b)
    vals[j] = jnp.minimum(a, b)


def _bitonic_sort_desc(vals):
    n = len(vals)
    k = 2
    while k <= n:
        j = k // 2
        while j >= 1:
            for i in range(n):
                l = i ^ j
                if l > i:
                    if (i & k) == 0:
                        _ce(vals, i, l)
                    else:
                        _ce(vals, l, i)
            j //= 2
        k *= 2


def _bitonic_merge_desc(vals):
    n = len(vals)
    j = n // 2
    while j >= 1:
        for i in range(n):
            l = i ^ j
            if l > i:
                _ce(vals, i, l)
        j //= 2


def _merge_top16(xs, ys):
    m = [jnp.maximum(xs[a], ys[P_TOPK - 1 - a]) for a in range(P_TOPK)]
    _bitonic_merge_desc(m)
    return m


def _top16_over_keys(vals):
    vals = list(vals)
    _bitonic_sort_desc(vals)
    for sh in (4, 2, 1):
        rolled = [pltpu.roll(v, sh, axis=0) for v in vals]
        vals = _merge_top16(vals, rolled)
    return vals


def _split_bf16(x):
    hi = x.astype(BF16)
    return hi, (x - hi.astype(F32)).astype(BF16)


def _route_kernel(xn_ref, wq_ref, k1_ref, k2_ref, cnt_ref, e1_ref, r2_ref, e2_ref):
    q_t = _dot_nt(wq_ref[...], xn_ref[...])
    k_parts = [_split_bf16(k1_ref[...]), _split_bf16(k2_ref[...])]
    s1_all, s2_all = [], []
    for h in range(P_HEADS):
        for half, dst in enumerate((s1_all, s2_all)):
            r0 = (2 * h + half) * P_KEY_DIM
            q_hi, q_lo = _split_bf16(q_t[r0:r0 + P_KEY_DIM, :])
            k_hi, k_lo = k_parts[half]
            main = _dot(k_hi, jnp.concatenate([q_hi, q_lo], axis=1))
            dst.append(main[:, 0:RT_T] + main[:, RT_T:2 * RT_T] + _dot(k_lo, q_hi))

    probs = [s for pair in zip(s1_all, s2_all) for s in pair]
    wide = [jnp.concatenate([s[kb * SUBLANES:(kb + 1) * SUBLANES, :] for s in probs], axis=1)
            for kb in range(P_NKEYS // SUBLANES)]
    tops = _top16_over_keys(wide)
    sub = lax.broadcasted_iota(jnp.int32, (SUBLANES, RT_T), 0)
    v1 = [None] * P_TOPK
    v2 = [None] * P_TOPK
    for a in range(P_TOPK):
        for h in range(P_HEADS):
            t1 = tops[a][:, (2 * h) * RT_T:(2 * h + 1) * RT_T]
            t2 = tops[a][:, (2 * h + 1) * RT_T:(2 * h + 2) * RT_T]
            v1[a] = t1 if h == 0 else jnp.where(sub == h, t1, v1[a])
            v2[a] = t2 if h == 0 else jnp.where(sub == h, t2, v2[a])
    pairs = [(a, b) for a in range(P_TOPK) for b in range(P_TOPK // (a + 1))]
    cand_of = {ab: v1[ab[0]] + v2[ab[1]] for ab in pairs}
    pad = jnp.full((SUBLANES, RT_T), NEG_INF, F32)
    cands = [cand_of[ab] for ab in pairs] + [pad] * (64 - len(pairs))
    groups = []
    for gi in range(4):
        grp = cands[gi::4]
        _bitonic_sort_desc(grp)
        groups.append(grp)
    top = _merge_top16(_merge_top16(groups[0], groups[1]), _merge_top16(groups[2], groups[3]))
    tau = top[P_TOPK - 1]
    z = jnp.ones_like(tau)
    for kk in range(1, P_TOPK):
        z = z + jnp.exp(top[kk] - top[0])
    zinv = 1.0 / z
    cnt_rank = []
    for a in range(P_TOPK):
        c_a = jnp.zeros((SUBLANES, RT_T), F32)
        for b in range(P_TOPK // (a + 1)):
            c_a = jnp.where(cand_of[(a, b)] >= tau, float(b + 1), c_a)
        cnt_rank.append(c_a)
    for h in range(P_HEADS):
        s1, s2 = s1_all[h], s2_all[h]
        cnt_rows = jnp.zeros((P_NKEYS, RT_T), F32)
        for a in range(P_TOPK):
            cnt_rows = jnp.where(s1 == v1[a][h:h + 1, :], cnt_rank[a][h:h + 1, :], cnt_rows)
        rank2 = jnp.zeros((P_NKEYS, RT_T), F32)
        for b in range(P_TOPK):
            rank2 = jnp.where(s2 < v2[b][h:h + 1, :], float(b + 1), rank2)
        cnt_ref[h] = cnt_rows
        e1_ref[h] = jnp.exp(s1 - v1[0][h:h + 1, :]) * zinv[h:h + 1, :]
        r2_ref[h] = pltpu.bitcast(rank2.astype(BF16), jnp.uint32)
        e2_ref[h] = pltpu.bitcast(jnp.exp(s2 - v2[0][h:h + 1, :]).astype(BF16), jnp.uint32)


def _route(xn3, w_pq_t, keys1, keys2):
    n = xn3.shape[0]
    full = lambda a: pl.BlockSpec(a.shape, lambda i: (0,) * a.ndim)
    oblk = lambda rows: pl.BlockSpec((P_HEADS, rows, RT_T), lambda i: (0, 0, i))
    f32_rows = jax.ShapeDtypeStruct((P_HEADS, P_NKEYS, n), F32)
    packed_rows = jax.ShapeDtypeStruct((P_HEADS, P_NKEYS // 2, n), jnp.uint32)
    return pl.pallas_call(
        _route_kernel,
        grid=(n // RT_T,),
        in_specs=[pl.BlockSpec((RT_T, D_MODEL), lambda i: (i, 0)), full(w_pq_t), full(keys1), full(keys2)],
        out_specs=[oblk(P_NKEYS), oblk(P_NKEYS), oblk(P_NKEYS // 2), oblk(P_NKEYS // 2)],
        out_shape=[f32_rows, f32_rows, packed_rows, packed_rows],
        compiler_params=_params(1, 32),
        name="route",
    )(xn3, w_pq_t, keys1, keys2)


PE_T = 1024
PE_E = 1024
_SQRT_HALF = 0.7071067811865476


PE_NE = P_EXPERTS // PE_E
PE_ROWS = PE_E // P_NKEYS
PE_ROW_GROUP = 1


def _peer_kernel(dn_ref, up_ref, xn_ref, cnt_ref, e1_ref, r2_ref, e2_ref, o_ref, a_buf, g_buf):
    s = pl.program_id(0)
    slot = lax.rem(s, 2)
    other = 1 - slot

    @pl.when(s == 0)
    def _():
        a_buf[1] = jnp.zeros((PE_E, PE_T), BF16)
        g_buf[0] = jnp.zeros((PE_E, PE_T), BF16)

    @pl.when((s == 0) | ((s >= 2) & (lax.rem(s - 2, PE_NE) == 0)))
    def _():
        o_ref[...] = jnp.zeros_like(o_ref)

    a_buf[slot] = _dot_nt(dn_ref[...], xn_ref[...]).astype(BF16)
    o_ref[...] += _dot(up_ref[...], g_buf[slot])

    n_slab = P_NKEYS // BF16_ROWS
    zero = jnp.zeros((), BF16)
    for c in range(PE_T // LANES):
        cs = slice(c * LANES, (c + 1) * LANES)
        for i0 in range(0, PE_ROWS, PE_ROW_GROUP):
            group = range(i0, i0 + PE_ROW_GROUP)
            coef = {i: [jnp.zeros((BF16_ROWS, LANES), BF16)] * n_slab for i in group}
            for h in range(P_HEADS):
                cnt_t = {i: jnp.broadcast_to(cnt_ref[h, 0, i:i + 1, cs], (BF16_ROWS, LANES)).astype(BF16)
                         for i in group}
                e1_t = {i: jnp.broadcast_to(e1_ref[h, 0, i:i + 1, cs], (BF16_ROWS, LANES)).astype(BF16)
                        for i in group}
                for k in range(n_slab):
                    js = slice(k * SUBLANES, (k + 1) * SUBLANES)
                    r2 = pltpu.bitcast(r2_ref[h, js, cs], BF16)
                    e2 = pltpu.bitcast(e2_ref[h, js, cs], BF16)
                    for i in group:
                        coef[i][k] = coef[i][k] + jnp.where(r2 < cnt_t[i], e2, zero) * e1_t[i]
            for i in group:
                for k in range(n_slab):
                    rows = slice(i * P_NKEYS + k * BF16_ROWS, i * P_NKEYS + (k + 1) * BF16_ROWS)
                    a = a_buf[other, rows, cs]
                    gl = 0.5 * a * (1.0 + lax.erf(a * _SQRT_HALF))
                    g_buf[other, rows, cs] = gl * coef[i][k]


def _peer(dn, up_t, xn3, cnt, e1, r2, e2):
    n = xn3.shape[0]
    assert PE_ROWS % SUBLANES == 0
    n_pairs = (n // PE_T) * PE_NE
    pair = lambda s, lag: jnp.clip(s - lag, 0, n_pairs - 1)
    tok = lambda s, lag: pair(s, lag) // PE_NE
    tile = lambda s, lag: pair(s, lag) % PE_NE
    cnt, e1 = (a.reshape(P_HEADS, PE_NE, PE_ROWS, n) for a in (cnt, e1))
    rblk = pl.BlockSpec((P_HEADS, 1, PE_ROWS, PE_T), lambda s: (0, tile(s, 1), 0, tok(s, 1)))
    pblk = pl.BlockSpec((P_HEADS, P_NKEYS // 2, PE_T), lambda s: (0, 0, tok(s, 1)))
    return pl.pallas_call(
        _peer_kernel,
        grid=(n_pairs + 2,),
        in_specs=[pl.BlockSpec((PE_E, D_MODEL), lambda s: (tile(s, 0), 0)),
                  pl.BlockSpec((D_MODEL, PE_E), lambda s: (0, tile(s, 2))),
                  pl.BlockSpec((PE_T, D_MODEL), lambda s: (tok(s, 0), 0)),
                  rblk, rblk, pblk, pblk],
        out_specs=pl.BlockSpec((D_MODEL, PE_T), lambda s: (0, tok(s, 2))),
        out_shape=jax.ShapeDtypeStruct((D_MODEL, n), F32),
        scratch_shapes=[pltpu.VMEM((2, PE_E, PE_T), BF16), pltpu.VMEM((2, PE_E, PE_T), BF16)],
        compiler_params=_params(1, 56),
        name="peer",
    )(dn, up_t, xn3, cnt, e1, r2, e2)


FIN_TM = 512


def _final_kernel(h_ref, pt_ref, g_ref, o_ref):
    o_ref[...] = _rms(h_ref[...] + pt_ref[...].T, g_ref[...])


def _final(h2, peer_t, g_final):
    n = h2.shape[0]
    return pl.pallas_call(
        _final_kernel,
        grid=(n // FIN_TM,),
        in_specs=[pl.BlockSpec((FIN_TM, D_MODEL), lambda i: (i, 0)),
                  pl.BlockSpec((D_MODEL, FIN_TM), lambda i: (0, i)),
                  pl.BlockSpec(g_final.shape, lambda i: (0, 0))],
        out_specs=pl.BlockSpec((FIN_TM, D_MODEL), lambda i: (i, 0)),
        out_shape=jax.ShapeDtypeStruct((n, D_MODEL), F32),
        compiler_params=_params(1, 32),
        name="final",
    )(h2, peer_t, g_final)


def _pack_w_in(w):
    o = 0
    segs = {}
    for name, sz in (("mq", 256), ("mk", 256), ("mv", 512), ("mo", 512), ("mi", 4), ("mf", 4),
                     ("aq", 512), ("ak", 128), ("av", 128)):
        segs[name] = w[:, o:o + sz]
        o += sz
    k0, k1 = segs["ak"][:, :64], segs["ak"][:, 64:]
    v0, v1 = segs["av"][:, :64], segs["av"][:, 64:]
    gates = jnp.concatenate([segs["mi"], segs["mf"], jnp.zeros((w.shape[0], LANES - 8), w.dtype)], axis=1)
    packed = jnp.concatenate([segs["mq"], segs["mk"], segs["aq"] * (A_DH ** -0.5), k0, k0, k1, k1, gates], axis=1)
    w_t = jnp.concatenate([segs["mv"], segs["mo"], v0, v0, v1, v1], axis=1).T
    return packed.astype(BF16), w_t.astype(BF16)


def kernel(x, mem, g_mix, w_in, conv_w, conv_b, b_igate, b_fgate, g_mhead, sinks, w_out, g_cross, g_mem,
           w_xq, w_xk, w_xv, w_xo, g_ffn, w_pq, sub_keys1, sub_keys2, expert_down, expert_up, g_final):
    assert x.shape == (BATCH, SEQ, D_MODEL) and w_in.shape[0] == 1
    l = 0
    row = lambda v: v.reshape(1, -1).astype(F32)
    x2 = x.reshape(NTOK, D_MODEL)

    qk_pre, aq, ak, gates, mv_t, mo_t, av_t = _mix_in(x2, row(g_mix[l]), *_pack_w_in(w_in[l]))

    gate_bias = jnp.concatenate([b_igate[l], b_fgate[l], jnp.zeros((LANES - 8,), F32)]).reshape(1, LANES)
    g_mhead_b = jnp.broadcast_to(g_mhead[l].astype(F32)[:, None], (M_HEADS * M_DV, LANES))
    r3 = lambda a: a.reshape(BATCH, SEQ, a.shape[-1])
    hm = _mlstm(r3(qk_pre), mv_t, mo_t, r3(gates), conv_w[l], row(conv_b[l]), gate_bias, g_mhead_b)

    sinks_b = jnp.broadcast_to(sinks[l].astype(F32)[:, None], (A_HEADS, LANES))
    ha = _swa(r3(aq), r3(ak), av_t, sinks_b)

    kmem, vmem = _memkv(mem, row(g_mem[l]), w_xk[l].astype(BF16), w_xv[l].astype(BF16))

    h2, xn3 = _post(x2, hm.reshape(NTOK, 512), ha.reshape(NTOK, 512), w_out[l].astype(BF16), row(g_cross[l]),
                    (w_xq[l] * (X_DH ** -0.5)).astype(BF16), kmem, vmem, w_xo[l].astype(BF16), row(g_ffn[l]))

    cnt, e1, r2, e2 = _route(xn3, w_pq[l].T.astype(BF16), sub_keys1[l], sub_keys2[l])

    peer_t = _peer(expert_down[l].astype(BF16), expert_up[l].T.astype(BF16), xn3, cnt, e1, r2, e2)

    out = _final(h2, peer_t, row(g_final))
    return out.reshape(BATCH, SEQ, D_MODEL)
```

```python
import jax
import jax.numpy as jnp
from jax import lax
from jax.experimental import pallas as pl
from jax.experimental.pallas import tpu as pltpu

F32 = jnp.float32
BF16 = jnp.bfloat16

D_MODEL = 1024
BATCH = 2
SEQ = 8192
NTOK = BATCH * SEQ
MEM_LEN = 256
EPS = 1e-6

M_HEADS = 4
M_DV = 128
M_DQK = 64
M_CONV = 4
M_QK_W = 2 * M_HEADS * M_DQK
M_V_W = M_HEADS * M_DV

A_HEADS = 8
A_KV_HEADS = 2
A_DH = 64
WINDOW = 128
A_BLOCK = 128
A_Q_W = A_HEADS * A_DH
A_KV_W = 2 * A_KV_HEADS * A_DH

X_HEADS = 4
X_DH = 256

P_HEADS = 8
P_NKEYS = 128
P_EXPERTS = P_NKEYS * P_NKEYS
P_KEY_DIM = 128
P_TOPK = 16

LANES = 128
HALF = LANES // 2
SUBLANES = 8
BF16_ROWS = 2 * SUBLANES

NEG_INF = float("-inf")

V7X_VMEM_BYTES = 64 * 1024 * 1024
MIB = 1024 * 1024


def _params(ndims, vmem_mib):
    assert vmem_mib * MIB < V7X_VMEM_BYTES
    return pltpu.CompilerParams(dimension_semantics=("arbitrary",) * ndims, vmem_limit_bytes=vmem_mib * MIB)


def _rms(xf, g):
    return xf * lax.rsqrt(jnp.mean(xf * xf, axis=-1, keepdims=True) + EPS) * g


def _dot_nt(a, b):
    return lax.dot_general(a, b, (((1,), (1,)), ((), ())), preferred_element_type=F32)


def _dot(a, b):
    return jnp.dot(a, b, preferred_element_type=F32)


MIX_TM = 512
MIX_COLS = (0, M_QK_W, M_QK_W + A_Q_W, M_QK_W + A_Q_W + A_KV_W, M_QK_W + A_Q_W + A_KV_W + LANES)
MIX_ROWS = (0, M_V_W, 2 * M_V_W, 2 * M_V_W + A_KV_W)


def _mix_in_kernel(x_ref, g_ref, w_ref, wt_ref, qk_ref, aq_ref, ak_ref, gt_ref, mvt_ref, mot_ref, avt_ref):
    xn = _rms(x_ref[...], g_ref[...]).astype(BF16)
    c, r = MIX_COLS, MIX_ROWS
    qk_ref[...] = _dot(xn, w_ref[:, c[0]:c[1]])
    aq_ref[...] = _dot(xn, w_ref[:, c[1]:c[2]]).astype(BF16)
    ak_ref[...] = _dot(xn, w_ref[:, c[2]:c[3]]).astype(BF16)
    gt_ref[...] = _dot(xn, w_ref[:, c[3]:c[4]])
    mvt_ref[...] = _dot_nt(wt_ref[r[0]:r[1], :], xn).astype(BF16)
    mot_ref[...] = _dot_nt(wt_ref[r[1]:r[2], :], xn)
    avt_ref[...] = _dot_nt(wt_ref[r[2]:r[3], :], xn).astype(BF16)


def _mix_in(x2, g_mix, w_all, w_t):
    n = x2.shape[0]
    row = lambda c: pl.BlockSpec((MIX_TM, c), lambda i: (i, 0))
    col = lambda r: pl.BlockSpec((r, MIX_TM), lambda i: (0, i))
    full = lambda a: pl.BlockSpec(a.shape, lambda i: (0,) * a.ndim)
    return pl.pallas_call(
        _mix_in_kernel,
        grid=(n // MIX_TM,),
        in_specs=[row(D_MODEL), full(g_mix), full(w_all), full(w_t)],
        out_specs=[row(M_QK_W), row(A_Q_W), row(A_KV_W), row(LANES), col(M_V_W), col(M_V_W), col(A_KV_W)],
        out_shape=[
            jax.ShapeDtypeStruct((n, M_QK_W), F32),
            jax.ShapeDtypeStruct((n, A_Q_W), BF16),
            jax.ShapeDtypeStruct((n, A_KV_W), BF16),
            jax.ShapeDtypeStruct((n, LANES), F32),
            jax.ShapeDtypeStruct((M_V_W, n), BF16),
            jax.ShapeDtypeStruct((M_V_W, n), F32),
            jax.ShapeDtypeStruct((A_KV_W, n), BF16),
        ],
        compiler_params=_params(1, 40),
        name="mix_in",
    )(x2, g_mix, w_all, w_t)


ML_T = 128
ML_HALO = 8


def _log_sigmoid(x):
    return jnp.minimum(x, 0.0) - jnp.log1p(jnp.exp(-jnp.abs(x)))


def _mlstm_kernel(qk_ref, halo_ref, mvt0_ref, mvt1_ref, mot0_ref, mot1_ref, gt_ref, cw_ref, cb_ref, gb_ref,
                  gmh_ref, hm_ref, xcat_ref, c_ref, n_ref, m_ref):
    step = pl.program_id(0)
    mvt_refs = (mvt0_ref, mvt1_ref)
    mot_refs = (mot0_ref, mot1_ref)

    @pl.when(step == 0)
    def _():
        c_ref[...] = jnp.zeros_like(c_ref)
        n_ref[...] = jnp.zeros_like(n_ref)
        m_ref[...] = jnp.zeros_like(m_ref)

    src = lax.broadcasted_iota(jnp.int32, (ML_T, ML_T), 0)
    dst = lax.broadcasted_iota(jnp.int32, (ML_T, ML_T), 1)
    lo = dst < HALF
    lo_row = lax.broadcasted_iota(jnp.int32, (1, LANES), 1) < HALF
    causal = src <= dst
    tri = (src >= dst).astype(F32)
    not_first = (step > 0).astype(F32)

    for b in range(BATCH):
        mvt_ref, mot_ref = mvt_refs[b], mot_refs[b]
        xcat_ref[b, 0:ML_HALO, :] = halo_ref[b] * not_first
        xcat_ref[b, ML_HALO:ML_HALO + ML_T, :] = qk_ref[b]
        acc = jnp.broadcast_to(cb_ref[...], (ML_T, M_QK_W))
        for j in range(M_CONV):
            acc = acc + cw_ref[j:j + 1, :] * xcat_ref[b, pl.ds(ML_HALO - (M_CONV - 1) + j, ML_T), :]
        qk = acc * jax.nn.sigmoid(acc)

        gb = gt_ref[b] + gb_ref[...]
        lf = _log_sigmoid(gb)
        bc = jnp.dot(tri, lf, precision=lax.Precision.HIGHEST,
                     preferred_element_type=F32)
        b_t = bc.T

        for p in range(M_HEADS // 2):
            qp = qk[:, p * LANES:(p + 1) * LANES] * (M_DQK ** -0.5)
            kp = qk[:, M_QK_W // 2 + p * LANES:M_QK_W // 2 + (p + 1) * LANES]
            kp_bf = kp.astype(BF16)
            nrow = n_ref[b * 2 + p][0:1, :]
            n_bf = jnp.broadcast_to(nrow, (SUBLANES, LANES)).astype(BF16)
            n_parts = []
            for hh in range(2):
                h = 2 * p + hh
                sidx = b * M_HEADS + h
                mh = lo if hh == 0 else jnp.logical_not(lo)
                qm_bf = jnp.where(mh, qp, 0.0).astype(BF16)
                s_t = _dot_nt(kp_bf, qm_bf)
                m_prev = m_ref[sidx][0:1, 0:1]
                b_row = b_t[M_HEADS + h:M_HEADS + h + 1, :]
                col = jnp.broadcast_to(gb[:, h:h + 1] - bc[:, M_HEADS + h:M_HEADS + h + 1], (ML_T, ML_T))
                log_d = jnp.where(causal, b_row + col, NEG_INF)
                inter = b_row + m_prev
                m_t = jnp.maximum(inter, jnp.max(log_d, axis=0, keepdims=True))
                sp = s_t * jnp.exp(log_d - m_t)
                w_inter = jnp.exp(inter - m_t)
                qn = _dot_nt(n_bf, qm_bf)[0:1, :]
                den = jnp.sum(sp, axis=0, keepdims=True) + w_inter * qn
                v_t = mvt_ref[h * M_DV:(h + 1) * M_DV, :]
                c_h = c_ref[sidx]
                num_t = _dot(v_t, sp.astype(BF16)) + w_inter * _dot_nt(c_h.astype(BF16), qm_bf)
                h_t = num_t / jnp.maximum(jnp.abs(den), jnp.exp(-m_t))
                h_n = h_t * lax.rsqrt(jnp.mean(h_t * h_t, axis=0, keepdims=True) + EPS)
                rows_h = slice(h * M_DV, (h + 1) * M_DV)
                out_t = jax.nn.sigmoid(mot_ref[rows_h, :]) * (h_n * gmh_ref[rows_h, :])
                hm_ref[b, :, h * M_DV:(h + 1) * M_DV] = out_t.T.astype(BF16)
                m_new = m_t[:, ML_T - 1:ML_T]
                b_last = b_row[:, ML_T - 1:ML_T]
                decay = jnp.exp(b_last + m_prev - m_new)
                wk = jnp.exp(col + (b_last - m_new)) * kp
                c_ref[sidx] = decay * c_h + jnp.where(mh, _dot(v_t, wk.astype(BF16)), 0.0)
                n_parts.append(decay * nrow + jnp.sum(wk, axis=0, keepdims=True))
                m_ref[sidx] = jnp.broadcast_to(m_new, (SUBLANES, LANES))
            n_new = jnp.where(lo_row, n_parts[0], n_parts[1])
            n_ref[b * 2 + p] = jnp.broadcast_to(n_new, (SUBLANES, LANES))


def _mlstm(qk_pre, mv_t, mo_t, gates, conv_w, conv_b, gate_bias, g_mhead_b):
    nblk = SEQ // ML_T
    blk = lambda c: pl.BlockSpec((BATCH, ML_T, c), lambda i: (0, i, 0))
    full = lambda a: pl.BlockSpec(a.shape, lambda i: (0,) * a.ndim)
    halo = pl.BlockSpec((BATCH, ML_HALO, M_QK_W),
                        lambda i: (0, jnp.maximum(i * (ML_T // ML_HALO) - 1, 0), 0))
    t_blk = lambda b: pl.BlockSpec((M_V_W, ML_T), lambda i: (0, b * nblk + i))
    return pl.pallas_call(
        _mlstm_kernel,
        grid=(nblk,),
        in_specs=[blk(M_QK_W), halo, t_blk(0), t_blk(1), t_blk(0), t_blk(1), blk(LANES),
                  full(conv_w), full(conv_b), full(gate_bias), full(g_mhead_b)],
        out_specs=blk(M_V_W),
        out_shape=jax.ShapeDtypeStruct((BATCH, SEQ, M_V_W), BF16),
        scratch_shapes=[
            pltpu.VMEM((BATCH, ML_HALO + ML_T, M_QK_W), F32),
            pltpu.VMEM((BATCH * M_HEADS, M_DV, LANES), F32),
            pltpu.VMEM((BATCH * 2, SUBLANES, LANES), F32),
            pltpu.VMEM((BATCH * M_HEADS, SUBLANES, LANES), F32),
        ],
        compiler_params=_params(1, 32),
        name="mlstm",
    )(qk_pre, qk_pre, mv_t, mv_t, mo_t, mo_t, gates, conv_w, conv_b, gate_bias, g_mhead_b)


SWA_SUB = 8
SWA_T = SWA_SUB * A_BLOCK


def _swa_kernel(q_ref, kp_ref, kc_ref, vtp_ref, vtc_ref, sk_ref, o_ref):
    n = pl.program_id(1)
    lo = lax.broadcasted_iota(jnp.int32, (A_BLOCK, LANES), 1) < HALF
    top = lax.broadcasted_iota(jnp.int32, (LANES, A_BLOCK), 0) < HALF
    kj = lax.broadcasted_iota(jnp.int32, (2 * A_BLOCK, A_BLOCK), 0)
    qi = lax.broadcasted_iota(jnp.int32, (2 * A_BLOCK, A_BLOCK), 1)
    diff = qi - kj + A_BLOCK
    band = (diff >= 0) & (diff < WINDOW)
    band_first = band & ((kj >= A_BLOCK) | (n > 0))
    for sb in range(SWA_SUB):
        q_rows = slice(sb * A_BLOCK, (sb + 1) * A_BLOCK)
        mask = band_first if sb == 0 else band
        for j in range(A_KV_HEADS):
            cols = slice(j * LANES, (j + 1) * LANES)
            if sb == 0:
                kk = jnp.concatenate([kp_ref[0, :, cols], kc_ref[0, 0:A_BLOCK, cols]], axis=0)
                vt = jnp.concatenate([vtp_ref[cols, :], vtc_ref[cols, 0:A_BLOCK]], axis=1)
            else:
                kv_rows = slice((sb - 1) * A_BLOCK, (sb + 1) * A_BLOCK)
                kk = kc_ref[0, kv_rows, cols]
                vt = vtc_ref[cols, kv_rows]
            for p in range(2):
                c0 = (2 * j + p) * LANES
                qp = q_ref[0, q_rows, c0:c0 + LANES].astype(F32)
                outs = []
                for hh in range(2):
                    h = 4 * j + 2 * p + hh
                    mh = lo if hh == 0 else jnp.logical_not(lo)
                    qm = jnp.where(mh, qp, 0.0).astype(BF16)
                    sc = jnp.where(mask, _dot_nt(kk, qm), NEG_INF)
                    sink = sk_ref[h:h + 1, 0:1]
                    mx = jnp.maximum(jnp.max(sc, axis=0, keepdims=True), sink)
                    pe = jnp.exp(sc - mx)
                    den = jnp.sum(pe, axis=0, keepdims=True) + jnp.exp(sink - mx)
                    outs.append(_dot(vt, pe.astype(BF16)) / den)
                o_t = jnp.where(top, outs[0], outs[1])
                o_ref[0, q_rows, c0:c0 + LANES] = o_t.T.astype(BF16)


def _swa(aq, ak, av_t, sinks_b):
    per_b = SEQ // SWA_T
    cur = lambda c: pl.BlockSpec((1, SWA_T, c), lambda b, n: (b, n, 0))
    prev = lambda c: pl.BlockSpec((1, A_BLOCK, c), lambda b, n: (b, jnp.maximum(n * SWA_SUB - 1, 0), 0))
    vt_cur = pl.BlockSpec((A_KV_W, SWA_T), lambda b, n: (0, b * per_b + n))
    vt_prev = pl.BlockSpec((A_KV_W, A_BLOCK),
                           lambda b, n: (0, b * per_b * SWA_SUB + jnp.maximum(n * SWA_SUB - 1, 0)))
    return pl.pallas_call(
        _swa_kernel,
        grid=(BATCH, per_b),
        in_specs=[cur(A_Q_W), prev(A_KV_W), cur(A_KV_W), vt_prev, vt_cur,
                  pl.BlockSpec(sinks_b.shape, lambda b, n: (0, 0))],
        out_specs=cur(A_Q_W),
        out_shape=jax.ShapeDtypeStruct((BATCH, SEQ, A_Q_W), BF16),
        compiler_params=_params(2, 32),
        name="swa",
    )(aq, ak, ak, av_t, av_t, sinks_b)


def _memkv_kernel(mem_ref, g_ref, wk_ref, wv_ref, k_ref, v_ref):
    mn = _rms(mem_ref[0], g_ref[...]).astype(BF16)
    k_ref[0] = _dot(mn, wk_ref[...]).astype(BF16)
    v_ref[0] = _dot(mn, wv_ref[...]).astype(BF16)


def _memkv(mem, g_mem, w_xk, w_xv):
    blk = pl.BlockSpec((1, MEM_LEN, D_MODEL), lambda b: (b, 0, 0))
    full = lambda a: pl.BlockSpec(a.shape, lambda b: (0,) * a.ndim)
    return pl.pallas_call(
        _memkv_kernel,
        grid=(BATCH,),
        in_specs=[blk, full(g_mem), full(w_xk), full(w_xv)],
        out_specs=[blk, blk],
        out_shape=[jax.ShapeDtypeStruct((BATCH, MEM_LEN, D_MODEL), BF16)] * 2,
        compiler_params=_params(1, 32),
        name="memkv",
    )(mem, g_mem, w_xk, w_xv)


POST_TM = 512


def _post_kernel(x_ref, hm_ref, ha_ref, wo_ref, gc_ref, wq_ref, km_ref, vm_ref, wxo_ref, gf_ref,
                 h2_ref, xn3_ref):
    h1 = x_ref[...] + _dot(hm_ref[...], wo_ref[0:M_V_W, :]) + _dot(ha_ref[...], wo_ref[M_V_W:M_V_W + A_Q_W, :])
    xn2 = _rms(h1, gc_ref[...]).astype(BF16)
    q = _dot(xn2, wq_ref[...]).astype(BF16)
    heads = []
    for hd in range(X_HEADS):
        cols = slice(hd * X_DH, (hd + 1) * X_DH)
        sc = _dot_nt(q[:, cols], km_ref[0, :, cols])
        mx = jnp.max(sc, axis=-1, keepdims=True)
        pe = jnp.exp(sc - mx)
        den = jnp.sum(pe, axis=-1, keepdims=True)
        heads.append((_dot(pe.astype(BF16), vm_ref[0, :, cols]) / den).astype(BF16))
    h2 = h1 + _dot(jnp.concatenate(heads, axis=1), wxo_ref[...])
    h2_ref[...] = h2
    xn3_ref[...] = _rms(h2, gf_ref[...]).astype(BF16)


def _post(x2, hm2, ha2, w_out, g_cross, w_xq, kmem, vmem, w_xo, g_ffn):
    n = x2.shape[0]
    per_batch = SEQ // POST_TM
    row = lambda c: pl.BlockSpec((POST_TM, c), lambda i: (i, 0))
    full = lambda a: pl.BlockSpec(a.shape, lambda i: (0,) * a.ndim)
    memblk = pl.BlockSpec((1, MEM_LEN, D_MODEL), lambda i: (i // per_batch, 0, 0))
    return pl.pallas_call(
        _post_kernel,
        grid=(n // POST_TM,),
        in_specs=[row(D_MODEL), row(M_V_W), row(A_Q_W), full(w_out), full(g_cross), full(w_xq),
                  memblk, memblk, full(w_xo), full(g_ffn)],
        out_specs=[row(D_MODEL), row(D_MODEL)],
        out_shape=[jax.ShapeDtypeStruct((n, D_MODEL), F32), jax.ShapeDtypeStruct((n, D_MODEL), BF16)],
        compiler_params=_params(1, 48),
        name="post",
    )(x2, hm2, ha2, w_out, g_cross, w_xq, kmem, vmem, w_xo, g_ffn)


RT_T = 512
RT_G = LANES


def _ce(vals, i, j):
    a, b = vals[i], vals[j]
    vals[i] = jnp.maximum(a, b)
    vals[j] = jnp.minimum(a, b)


def _bitonic_sort_desc(vals):
    n = len(vals)
    k = 2
    while k <= n:
        j = k // 2
        while j >= 1:
            for i in range(n):
                l = i ^ j
                if l > i:
                    if (i & k) == 0:
                        _ce(vals, i, l)
                    else:
                        _ce(vals, l, i)
            j //= 2
        k *= 2


def _bitonic_merge_desc(vals):
    n = len(vals)
    j = n // 2
    while j >= 1:
        for i in range(n):
            l = i ^ j
            if l > i:
                _ce(vals, i, l)
        j //= 2


def _merge_top16(xs, ys):
    m = [jnp.maximum(xs[a], ys[P_TOPK - 1 - a]) for a in range(P_TOPK)]
    _bitonic_merge_desc(m)
    return m


def _top16_over_keys(vals):
    vals = list(vals)
    _bitonic_sort_desc(vals)
    for sh in (4, 2, 1):
        rolled = [pltpu.roll(v, sh, axis=0) for v in vals]
        vals = _merge_top16(vals, rolled)
    return vals


def _split_bf16(x):
    hi = x.astype(BF16)
    return hi, (x - hi.astype(F32)).astype(BF16)


def _route_kernel(xn_ref, wq_ref, k1_ref, k2_ref, cnt_ref, e1_ref, r2_ref, e2_ref):
    q_t = _dot_nt(wq_ref[...], xn_ref[...])
    k_parts = [_split_bf16(k1_ref[...]), _split_bf16(k2_ref[...])]
    s1_all, s2_all = [], []
    for h in range(P_HEADS):
        for half, dst in enumerate((s1_all, s2_all)):
            r0 = (2 * h + half) * P_KEY_DIM
            q_hi, q_lo = _split_bf16(q_t[r0:r0 + P_KEY_DIM, :])
            k_hi, k_lo = k_parts[half]
            main = _dot(k_hi, jnp.concatenate([q_hi, q_lo], axis=1))
            dst.append(main[:, 0:RT_T] + main[:, RT_T:2 * RT_T] + _dot(k_lo, q_hi))
    for g in range(RT_T // RT_G):
        ls = slice(g * RT_G, (g + 1) * RT_G)
        _route_select([s[:, ls] for s in s1_all], [s[:, ls] for s in s2_all], cnt_ref, e1_ref, r2_ref, e2_ref, ls)


def _route_select(s1_all, s2_all, cnt_ref, e1_ref, r2_ref, e2_ref, ls):
    probs = [s for pair in zip(s1_all, s2_all) for s in pair]
    wide = [jnp.concatenate([s[kb * SUBLANES:(kb + 1) * SUBLANES, :] for s in probs], axis=1)
            for kb in range(P_NKEYS // SUBLANES)]
    tops = _top16_over_keys(wide)
    sub = lax.broadcasted_iota(jnp.int32, (SUBLANES, RT_G), 0)
    v1 = [None] * P_TOPK
    v2 = [None] * P_TOPK
    for a in range(P_TOPK):
        for h in range(P_HEADS):
            t1 = tops[a][:, (2 * h) * RT_G:(2 * h + 1) * RT_G]
            t2 = tops[a][:, (2 * h + 1) * RT_G:(2 * h + 2) * RT_G]
            v1[a] = t1 if h == 0 else jnp.where(sub == h, t1, v1[a])
            v2[a] = t2 if h == 0 else jnp.where(sub == h, t2, v2[a])
    pairs = [(a, b) for a in range(P_TOPK) for b in range(P_TOPK // (a + 1))]
    cand_of = {ab: v1[ab[0]] + v2[ab[1]] for ab in pairs}
    pad = jnp.full((SUBLANES, RT_G), NEG_INF, F32)
    cands = [cand_of[ab] for ab in pairs] + [pad] * (64 - len(pairs))
    groups = []
    for gi in range(4):
        grp = cands[gi::4]
        _bitonic_sort_desc(grp)
        groups.append(grp)
    top = _merge_top16(_merge_top16(groups[0], groups[1]), _merge_top16(groups[2], groups[3]))
    tau = top[P_TOPK - 1]
    z = jnp.ones_like(tau)
    for kk in range(1, P_TOPK):
        z = z + jnp.exp(top[kk] - top[0])
    zinv = 1.0 / z
    cnt_rank = []
    for a in range(P_TOPK):
        c_a = jnp.zeros((SUBLANES, RT_G), F32)
        for b in range(P_TOPK // (a + 1)):
            c_a = jnp.where(cand_of[(a, b)] >= tau, float(b + 1), c_a)
        cnt_rank.append(c_a)
    for h in range(P_HEADS):
        s1, s2 = s1_all[h], s2_all[h]
        cnt_rows = jnp.zeros((P_NKEYS, RT_G), F32)
        for a in range(P_TOPK):
            cnt_rows = jnp.where(s1 == v1[a][h:h + 1, :], cnt_rank[a][h:h + 1, :], cnt_rows)
        rank2 = jnp.zeros((P_NKEYS, RT_G), F32)
        for b in range(P_TOPK):
            rank2 = jnp.where(s2 < v2[b][h:h + 1, :], float(b + 1), rank2)
        cnt_ref[h, :, ls] = cnt_rows
        e1_ref[h, :, ls] = jnp.exp(s1 - v1[0][h:h + 1, :]) * zinv[h:h + 1, :]
        r2_ref[h, :, ls] = pltpu.bitcast(rank2.astype(BF16), jnp.uint32)
        e2_ref[h, :, ls] = pltpu.bitcast(jnp.exp(s2 - v2[0][h:h + 1, :]).astype(BF16), jnp.uint32)


def _route(xn3, w_pq_t, keys1, keys2):
    n = xn3.shape[0]
    full = lambda a: pl.BlockSpec(a.shape, lambda i: (0,) * a.ndim)
    oblk = lambda rows: pl.BlockSpec((P_HEADS, rows, RT_T), lambda i: (0, 0, i))
    f32_rows = jax.ShapeDtypeStruct((P_HEADS, P_NKEYS, n), F32)
    packed_rows = jax.ShapeDtypeStruct((P_HEADS, P_NKEYS // 2, n), jnp.uint32)
    return pl.pallas_call(
        _route_kernel,
        grid=(n // RT_T,),
        in_specs=[pl.BlockSpec((RT_T, D_MODEL), lambda i: (i, 0)), full(w_pq_t), full(keys1), full(keys2)],
        out_specs=[oblk(P_NKEYS), oblk(P_NKEYS), oblk(P_NKEYS // 2), oblk(P_NKEYS // 2)],
        out_shape=[f32_rows, f32_rows, packed_rows, packed_rows],
        compiler_params=_params(1, 40),
        name="route",
    )(xn3, w_pq_t, keys1, keys2)


PE_T = 1024
PE_E = 1024
PE_NE = P_EXPERTS // PE_E
PE_ROWS = PE_E // P_NKEYS
_SQRT_HALF = 0.7071067811865476


def _peer_kernel(dn_ref, up_ref, xn_ref, cnt_ref, e1_ref, r2_ref, e2_ref, o_ref, a_buf, g_buf):
    s = pl.program_id(0)
    slot = lax.rem(s, 2)
    other = 1 - slot

    @pl.when(s == 0)
    def _():
        a_buf[1] = jnp.zeros((PE_E, PE_T), BF16)
        g_buf[0] = jnp.zeros((PE_E, PE_T), BF16)

    @pl.when((s == 0) | ((s >= 2) & (lax.rem(s - 2, PE_NE) == 0)))
    def _():
        o_ref[...] = jnp.zeros_like(o_ref)

    a_buf[slot] = _dot_nt(dn_ref[...], xn_ref[...]).astype(BF16)
    o_ref[...] += _dot(up_ref[...], g_buf[slot])

    n_slab = P_NKEYS // BF16_ROWS
    zero = jnp.zeros((), BF16)
    for c in range(PE_T // LANES):
        cs = slice(c * LANES, (c + 1) * LANES)
        for i in range(PE_ROWS):
            coef = [jnp.zeros((BF16_ROWS, LANES), BF16)] * n_slab
            for h in range(P_HEADS):
                cnt_t = jnp.broadcast_to(cnt_ref[h, 0, i:i + 1, cs], (BF16_ROWS, LANES)).astype(BF16)
                e1_t = jnp.broadcast_to(e1_ref[h, 0, i:i + 1, cs], (BF16_ROWS, LANES)).astype(BF16)
                for k in range(n_slab):
                    js = slice(k * SUBLANES, (k + 1) * SUBLANES)
                    r2 = pltpu.bitcast(r2_ref[h, js, cs], BF16)
                    e2 = pltpu.bitcast(e2_ref[h, js, cs], BF16)
                    coef[k] = coef[k] + jnp.where(r2 < cnt_t, e2, zero) * e1_t
            for k in range(n_slab):
                rows = slice(i * P_NKEYS + k * BF16_ROWS, i * P_NKEYS + (k + 1) * BF16_ROWS)
                a = a_buf[other, rows, cs]
                gl = 0.5 * a * (1.0 + lax.erf(a * _SQRT_HALF))
                g_buf[other, rows, cs] = gl * coef[k]


def _peer(dn, up_t, xn3, cnt, e1, r2, e2):
    n = xn3.shape[0]
    assert PE_ROWS % SUBLANES == 0
    n_pairs = (n // PE_T) * PE_NE
    pair = lambda s, lag: jnp.clip(s - lag, 0, n_pairs - 1)
    tok = lambda s, lag: pair(s, lag) // PE_NE
    tile = lambda s, lag: pair(s, lag) % PE_NE
    cnt, e1 = (a.reshape(P_HEADS, PE_NE, PE_ROWS, n) for a in (cnt, e1))
    rblk = pl.BlockSpec((P_HEADS, 1, PE_ROWS, PE_T), lambda s: (0, tile(s, 1), 0, tok(s, 1)))
    pblk = pl.BlockSpec((P_HEADS, P_NKEYS // 2, PE_T), lambda s: (0, 0, tok(s, 1)))
    return pl.pallas_call(
        _peer_kernel,
        grid=(n_pairs + 2,),
        in_specs=[pl.BlockSpec((PE_E, D_MODEL), lambda s: (tile(s, 0), 0)),
                  pl.BlockSpec((D_MODEL, PE_E), lambda s: (0, tile(s, 2))),
                  pl.BlockSpec((PE_T, D_MODEL), lambda s: (tok(s, 0), 0)),
                  rblk, rblk, pblk, pblk],
        out_specs=pl.BlockSpec((D_MODEL, PE_T), lambda s: (0, tok(s, 2))),
        out_shape=jax.ShapeDtypeStruct((D_MODEL, n), F32),
        scratch_shapes=[pltpu.VMEM((2, PE_E, PE_T), BF16), pltpu.VMEM((2, PE_E, PE_T), BF16)],
        compiler_params=_params(1, 56),
        name="peer",
    )(dn, up_t, xn3, cnt, e1, r2, e2)


FIN_TM = 512


def _final_kernel(h_ref, pt_ref, g_ref, o_ref):
    o_ref[...] = _rms(h_ref[...] + pt_ref[...].T, g_ref[...])


def _final(h2, peer_t, g_final):
    n = h2.shape[0]
    return pl.pallas_call(
        _final_kernel,
        grid=(n // FIN_TM,),
        in_specs=[pl.BlockSpec((FIN_TM, D_MODEL), lambda i: (i, 0)),
                  pl.BlockSpec((D_MODEL, FIN_TM), lambda i: (0, i)),
                  pl.BlockSpec(g_final.shape, lambda i: (0, 0))],
        out_specs=pl.BlockSpec((FIN_TM, D_MODEL), lambda i: (i, 0)),
        out_shape=jax.ShapeDtypeStruct((n, D_MODEL), F32),
        compiler_params=_params(1, 32),
        name="final",
    )(h2, peer_t, g_final)


def _pack_w_in(w):
    o = 0
    segs = {}
    for name, sz in (("mq", M_QK_W // 2), ("mk", M_QK_W // 2), ("mv", M_V_W), ("mo", M_V_W),
                     ("mi", M_HEADS), ("mf", M_HEADS), ("aq", A_Q_W), ("ak", A_KV_HEADS * A_DH),
                     ("av", A_KV_HEADS * A_DH)):
        segs[name] = w[:, o:o + sz]
        o += sz
    k0, k1 = segs["ak"][:, :A_DH], segs["ak"][:, A_DH:]
    v0, v1 = segs["av"][:, :A_DH], segs["av"][:, A_DH:]
    gates = jnp.concatenate([segs["mi"], segs["mf"], jnp.zeros((w.shape[0], LANES - 2 * M_HEADS), w.dtype)], axis=1)
    packed = jnp.concatenate([segs["mq"], segs["mk"], segs["aq"] * (A_DH ** -0.5), k0, k0, k1, k1, gates], axis=1)
    w_t = jnp.concatenate([segs["mv"], segs["mo"], v0, v0, v1, v1], axis=1).T
    return packed.astype(BF16), w_t.astype(BF16)


def kernel(x, mem, g_mix, w_in, conv_w, conv_b, b_igate, b_fgate, g_mhead, sinks, w_out, g_cross, g_mem,
           w_xq, w_xk, w_xv, w_xo, g_ffn, w_pq, sub_keys1, sub_keys2, expert_down, expert_up, g_final):
    assert x.shape == (BATCH, SEQ, D_MODEL) and w_in.shape[0] == 1
    l = 0
    row = lambda v: v.reshape(1, -1).astype(F32)
    x2 = x.reshape(NTOK, D_MODEL)

    qk_pre, aq, ak, gates, mv_t, mo_t, av_t = _mix_in(x2, row(g_mix[l]), *_pack_w_in(w_in[l]))

    gate_bias = jnp.concatenate([b_igate[l], b_fgate[l], jnp.zeros((LANES - 2 * M_HEADS,), F32)]).reshape(1, LANES)
    g_mhead_b = jnp.broadcast_to(g_mhead[l].astype(F32)[:, None], (M_V_W, LANES))
    r3 = lambda a: a.reshape(BATCH, SEQ, a.shape[-1])
    hm = _mlstm(r3(qk_pre), mv_t, mo_t, r3(gates), conv_w[l], row(conv_b[l]), gate_bias, g_mhead_b)

    sinks_b = jnp.broadcast_to(sinks[l].astype(F32)[:, None], (A_HEADS, LANES))
    ha = _swa(r3(aq), r3(ak), av_t, sinks_b)

    kmem, vmem = _memkv(mem, row(g_mem[l]), w_xk[l].astype(BF16), w_xv[l].astype(BF16))

    h2, xn3 = _post(x2, hm.reshape(NTOK, M_V_W), ha.reshape(NTOK, A_Q_W), w_out[l].astype(BF16), row(g_cross[l]),
                    (w_xq[l] * (X_DH ** -0.5)).astype(BF16), kmem, vmem, w_xo[l].astype(BF16), row(g_ffn[l]))

    cnt, e1, r2, e2 = _route(xn3, w_pq[l].T.astype(BF16), sub_keys1[l], sub_keys2[l])

    peer_t = _peer(expert_down[l].astype(BF16), expert_up[l].T.astype(BF16), xn3, cnt, e1, r2, e2)

    out = _final(h2, peer_t, row(g_final))
    return out.reshape(BATCH, SEQ, D_MODEL)
```

```python
import jax
import jax.numpy as jnp
from jax import lax
from jax.experimental import pallas as pl
from jax.experimental.pallas import tpu as pltpu

F32 = jnp.float32
BF16 = jnp.bfloat16

D_MODEL = 1024
BATCH = 2
SEQ = 8192
NTOK = BATCH * SEQ
MEM_LEN = 256
EPS = 1e-6

M_HEADS = 4
M_DV = 128
M_DQK = 64
M_CONV = 4
M_QK_W = 2 * M_HEADS * M_DQK
M_V_W = M_HEADS * M_DV

A_HEADS = 8
A_KV_HEADS = 2
A_DH = 64
WINDOW = 128
A_BLOCK = 128
A_Q_W = A_HEADS * A_DH
A_KV_W = 2 * A_KV_HEADS * A_DH

X_HEADS = 4
X_DH = 256

P_HEADS = 8
P_NKEYS = 128
P_EXPERTS = P_NKEYS * P_NKEYS
P_KEY_DIM = 128
P_TOPK = 16

LANES = 128
HALF = LANES // 2
SUBLANES = 8
BF16_ROWS = 2 * SUBLANES

NEG_INF = float("-inf")

V7X_VMEM_BYTES = 64 * 1024 * 1024
MIB = 1024 * 1024


def _params(ndims, vmem_mib):
    assert vmem_mib * MIB < V7X_VMEM_BYTES
    return pltpu.CompilerParams(dimension_semantics=("arbitrary",) * ndims, vmem_limit_bytes=vmem_mib * MIB)


def _rms(xf, g):
    return xf * lax.rsqrt(jnp.mean(xf * xf, axis=-1, keepdims=True) + EPS) * g


def _dot_nt(a, b):
    return lax.dot_general(a, b, (((1,), (1,)), ((), ())), preferred_element_type=F32)


def _dot(a, b):
    return jnp.dot(a, b, preferred_element_type=F32)


MIX_TM = 512
MIX_COLS = (0, M_QK_W, M_QK_W + A_Q_W, M_QK_W + A_Q_W + A_KV_W, M_QK_W + A_Q_W + A_KV_W + LANES)
MIX_ROWS = (0, M_V_W, 2 * M_V_W, 2 * M_V_W + A_KV_W)


def _mix_in_kernel(x_ref, g_ref, w_ref, wt_ref, qk_ref, aq_ref, ak_ref, gt_ref, mvt_ref, mot_ref, avt_ref):
    xn = _rms(x_ref[...], g_ref[...]).astype(BF16)
    c, r = MIX_COLS, MIX_ROWS
    qk_ref[...] = _dot(xn, w_ref[:, c[0]:c[1]])
    aq_ref[...] = _dot(xn, w_ref[:, c[1]:c[2]]).astype(BF16)
    ak_ref[...] = _dot(xn, w_ref[:, c[2]:c[3]]).astype(BF16)
    gt_ref[...] = _dot(xn, w_ref[:, c[3]:c[4]])
    mvt_ref[...] = _dot_nt(wt_ref[r[0]:r[1], :], xn).astype(BF16)
    mot_ref[...] = _dot_nt(wt_ref[r[1]:r[2], :], xn)
    avt_ref[...] = _dot_nt(wt_ref[r[2]:r[3], :], xn).astype(BF16)


def _mix_in(x2, g_mix, w_all, w_t):
    n = x2.shape[0]
    row = lambda c: pl.BlockSpec((MIX_TM, c), lambda i: (i, 0))
    col = lambda r: pl.BlockSpec((r, MIX_TM), lambda i: (0, i))
    full = lambda a: pl.BlockSpec(a.shape, lambda i: (0,) * a.ndim)
    return pl.pallas_call(
        _mix_in_kernel,
        grid=(n // MIX_TM,),
        in_specs=[row(D_MODEL), full(g_mix), full(w_all), full(w_t)],
        out_specs=[row(M_QK_W), row(A_Q_W), row(A_KV_W), row(LANES), col(M_V_W), col(M_V_W), col(A_KV_W)],
        out_shape=[
            jax.ShapeDtypeStruct((n, M_QK_W), F32),
            jax.ShapeDtypeStruct((n, A_Q_W), BF16),
            jax.ShapeDtypeStruct((n, A_KV_W), BF16),
            jax.ShapeDtypeStruct((n, LANES), F32),
            jax.ShapeDtypeStruct((M_V_W, n), BF16),
            jax.ShapeDtypeStruct((M_V_W, n), F32),
            jax.ShapeDtypeStruct((A_KV_W, n), BF16),
        ],
        compiler_params=_params(1, 40),
        name="mix_in",
    )(x2, g_mix, w_all, w_t)


ML_T = 128
ML_SUB = 2
ML_STEP = ML_SUB * ML_T
ML_HALO = 8


def _log_sigmoid(x):
    return jnp.minimum(x, 0.0) - jnp.log1p(jnp.exp(-jnp.abs(x)))


def _mlstm_kernel(qk_ref, halo_ref, mvt0_ref, mvt1_ref, mot0_ref, mot1_ref, gt_ref, cw_ref, cb_ref, gb_ref,
                  gmh_ref, hm_ref, xcat_ref, c_ref, n_ref, m_ref):
    step = pl.program_id(0)
    mvt_refs = (mvt0_ref, mvt1_ref)
    mot_refs = (mot0_ref, mot1_ref)

    @pl.when(step == 0)
    def _():
        c_ref[...] = jnp.zeros_like(c_ref)
        n_ref[...] = jnp.zeros_like(n_ref)
        m_ref[...] = jnp.zeros_like(m_ref)

    src = lax.broadcasted_iota(jnp.int32, (ML_T, ML_T), 0)
    dst = lax.broadcasted_iota(jnp.int32, (ML_T, ML_T), 1)
    lo = dst < HALF
    lo_row = lax.broadcasted_iota(jnp.int32, (1, LANES), 1) < HALF
    causal = src <= dst
    tri = (src >= dst).astype(F32)
    not_first = (step > 0).astype(F32)

    for b in range(BATCH):
        mvt_ref, mot_ref = mvt_refs[b], mot_refs[b]
        xcat_ref[b, 0:ML_HALO, :] = halo_ref[b] * not_first
        xcat_ref[b, ML_HALO:ML_HALO + ML_STEP, :] = qk_ref[b]
        acc = jnp.broadcast_to(cb_ref[...], (ML_STEP, M_QK_W))
        for j in range(M_CONV):
            acc = acc + cw_ref[j:j + 1, :] * xcat_ref[b, pl.ds(ML_HALO - (M_CONV - 1) + j, ML_STEP), :]
        qk_step = acc * jax.nn.sigmoid(acc)

        gb_step = gt_ref[b] + gb_ref[...]
        lf_step = _log_sigmoid(gb_step)
        chunks = [slice(u * ML_T, (u + 1) * ML_T) for u in range(ML_SUB)]
        bcs = [jnp.dot(tri, lf_step[tok], precision=lax.Precision.HIGHEST, preferred_element_type=F32)
               for tok in chunks]
        b_ts = [bc.T for bc in bcs]
        for u, p in [(u, p) for u in range(ML_SUB) for p in range(M_HEADS // 2)]:
            tok, bc, b_t = chunks[u], bcs[u], b_ts[u]
            qk, gb = qk_step[tok], gb_step[tok]
            qp = qk[:, p * LANES:(p + 1) * LANES] * (M_DQK ** -0.5)
            kp = qk[:, M_QK_W // 2 + p * LANES:M_QK_W // 2 + (p + 1) * LANES]
            kp_bf = kp.astype(BF16)
            nrow = n_ref[b * 2 + p][0:1, :]
            n_bf = jnp.broadcast_to(nrow, (SUBLANES, LANES)).astype(BF16)
            n_parts = []
            for hh in range(2):
                h = 2 * p + hh
                sidx = b * M_HEADS + h
                mh = lo if hh == 0 else jnp.logical_not(lo)
                qm_bf = jnp.where(mh, qp, 0.0).astype(BF16)
                s_t = _dot_nt(kp_bf, qm_bf)
                m_prev = m_ref[sidx][0:1, 0:1]
                b_row = b_t[M_HEADS + h:M_HEADS + h + 1, :]
                col = jnp.broadcast_to(gb[:, h:h + 1] - bc[:, M_HEADS + h:M_HEADS + h + 1], (ML_T, ML_T))
                log_d = jnp.where(causal, b_row + col, NEG_INF)
                inter = b_row + m_prev
                m_t = jnp.maximum(inter, jnp.max(log_d, axis=0, keepdims=True))
                sp = s_t * jnp.exp(log_d - m_t)
                w_inter = jnp.exp(inter - m_t)
                qn = _dot_nt(n_bf, qm_bf)[0:1, :]
                den = jnp.sum(sp, axis=0, keepdims=True) + w_inter * qn
                v_t = mvt_ref[h * M_DV:(h + 1) * M_DV, tok]
                c_h = c_ref[sidx]
                num_t = _dot(v_t, sp.astype(BF16)) + w_inter * _dot_nt(c_h.astype(BF16), qm_bf)
                h_t = num_t / jnp.maximum(jnp.abs(den), jnp.exp(-m_t))
                h_n = h_t * lax.rsqrt(jnp.mean(h_t * h_t, axis=0, keepdims=True) + EPS)
                rows_h = slice(h * M_DV, (h + 1) * M_DV)
                out_t = jax.nn.sigmoid(mot_ref[rows_h, tok]) * (h_n * gmh_ref[rows_h, :])
                hm_ref[b, tok, h * M_DV:(h + 1) * M_DV] = out_t.T.astype(BF16)
                m_new = m_t[:, ML_T - 1:ML_T]
                b_last = b_row[:, ML_T - 1:ML_T]
                decay = jnp.exp(b_last + m_prev - m_new)
                wk = jnp.exp(col + (b_last - m_new)) * kp
                c_ref[sidx] = decay * c_h + jnp.where(mh, _dot(v_t, wk.astype(BF16)), 0.0)
                n_parts.append(decay * nrow + jnp.sum(wk, axis=0, keepdims=True))
                m_ref[sidx] = jnp.broadcast_to(m_new, (SUBLANES, LANES))
            n_new = jnp.where(lo_row, n_parts[0], n_parts[1])
            n_ref[b * 2 + p] = jnp.broadcast_to(n_new, (SUBLANES, LANES))


def _mlstm(qk_pre, mv_t, mo_t, gates, conv_w, conv_b, gate_bias, g_mhead_b):
    nblk = SEQ // ML_STEP
    blk = lambda c: pl.BlockSpec((BATCH, ML_STEP, c), lambda i: (0, i, 0))
    full = lambda a: pl.BlockSpec(a.shape, lambda i: (0,) * a.ndim)
    halo = pl.BlockSpec((BATCH, ML_HALO, M_QK_W),
                        lambda i: (0, jnp.maximum(i * (ML_STEP // ML_HALO) - 1, 0), 0))
    t_blk = lambda b: pl.BlockSpec((M_V_W, ML_STEP), lambda i: (0, b * nblk + i))
    return pl.pallas_call(
        _mlstm_kernel,
        grid=(nblk,),
        in_specs=[blk(M_QK_W), halo, t_blk(0), t_blk(1), t_blk(0), t_blk(1), blk(LANES),
                  full(conv_w), full(conv_b), full(gate_bias), full(g_mhead_b)],
        out_specs=blk(M_V_W),
        out_shape=jax.ShapeDtypeStruct((BATCH, SEQ, M_V_W), BF16),
        scratch_shapes=[
            pltpu.VMEM((BATCH, ML_HALO + ML_STEP, M_QK_W), F32),
            pltpu.VMEM((BATCH * M_HEADS, M_DV, LANES), F32),
            pltpu.VMEM((BATCH * 2, SUBLANES, LANES), F32),
            pltpu.VMEM((BATCH * M_HEADS, SUBLANES, LANES), F32),
        ],
        compiler_params=_params(1, 32),
        name="mlstm",
    )(qk_pre, qk_pre, mv_t, mv_t, mo_t, mo_t, gates, conv_w, conv_b, gate_bias, g_mhead_b)


SWA_SUB = 8
SWA_T = SWA_SUB * A_BLOCK


def _swa_kernel(q_ref, kp_ref, kc_ref, vtp_ref, vtc_ref, sk_ref, o_ref):
    n = pl.program_id(1)
    lo = lax.broadcasted_iota(jnp.int32, (A_BLOCK, LANES), 1) < HALF
    top = lax.broadcasted_iota(jnp.int32, (LANES, A_BLOCK), 0) < HALF
    kj = lax.broadcasted_iota(jnp.int32, (2 * A_BLOCK, A_BLOCK), 0)
    qi = lax.broadcasted_iota(jnp.int32, (2 * A_BLOCK, A_BLOCK), 1)
    diff = qi - kj + A_BLOCK
    band = (diff >= 0) & (diff < WINDOW)
    band_first = band & ((kj >= A_BLOCK) | (n > 0))
    for sb in range(SWA_SUB):
        q_rows = slice(sb * A_BLOCK, (sb + 1) * A_BLOCK)
        mask = band_first if sb == 0 else band
        for j in range(A_KV_HEADS):
            cols = slice(j * LANES, (j + 1) * LANES)
            if sb == 0:
                kk = jnp.concatenate([kp_ref[0, :, cols], kc_ref[0, 0:A_BLOCK, cols]], axis=0)
                vt = jnp.concatenate([vtp_ref[cols, :], vtc_ref[cols, 0:A_BLOCK]], axis=1)
            else:
                kv_rows = slice((sb - 1) * A_BLOCK, (sb + 1) * A_BLOCK)
                kk = kc_ref[0, kv_rows, cols]
                vt = vtc_ref[cols, kv_rows]
            for p in range(2):
                c0 = (2 * j + p) * LANES
                qp = q_ref[0, q_rows, c0:c0 + LANES].astype(F32)
                outs = []
                for hh in range(2):
                    h = 4 * j + 2 * p + hh
                    mh = lo if hh == 0 else jnp.logical_not(lo)
                    qm = jnp.where(mh, qp, 0.0).astype(BF16)
                    sc = jnp.where(mask, _dot_nt(kk, qm), NEG_INF)
                    sink = sk_ref[h:h + 1, 0:1]
                    mx = jnp.maximum(jnp.max(sc, axis=0, keepdims=True), sink)
                    pe = jnp.exp(sc - mx)
                    den = jnp.sum(pe, axis=0, keepdims=True) + jnp.exp(sink - mx)
                    outs.append(_dot(vt, pe.astype(BF16)) / den)
                o_t = jnp.where(top, outs[0], outs[1])
                o_ref[0, q_rows, c0:c0 + LANES] = o_t.T.astype(BF16)


def _swa(aq, ak, av_t, sinks_b):
    per_b = SEQ // SWA_T
    cur = lambda c: pl.BlockSpec((1, SWA_T, c), lambda b, n: (b, n, 0))
    prev = lambda c: pl.BlockSpec((1, A_BLOCK, c), lambda b, n: (b, jnp.maximum(n * SWA_SUB - 1, 0), 0))
    vt_cur = pl.BlockSpec((A_KV_W, SWA_T), lambda b, n: (0, b * per_b + n))
    vt_prev = pl.BlockSpec((A_KV_W, A_BLOCK),
                           lambda b, n: (0, b * per_b * SWA_SUB + jnp.maximum(n * SWA_SUB - 1, 0)))
    return pl.pallas_call(
        _swa_kernel,
        grid=(BATCH, per_b),
        in_specs=[cur(A_Q_W), prev(A_KV_W), cur(A_KV_W), vt_prev, vt_cur,
                  pl.BlockSpec(sinks_b.shape, lambda b, n: (0, 0))],
        out_specs=cur(A_Q_W),
        out_shape=jax.ShapeDtypeStruct((BATCH, SEQ, A_Q_W), BF16),
        compiler_params=_params(2, 32),
        name="swa",
    )(aq, ak, ak, av_t, av_t, sinks_b)


def _memkv_kernel(mem_ref, g_ref, wk_ref, wv_ref, k_ref, v_ref):
    mn = _rms(mem_ref[0], g_ref[...]).astype(BF16)
    k_ref[0] = _dot(mn, wk_ref[...]).astype(BF16)
    v_ref[0] = _dot(mn, wv_ref[...]).astype(BF16)


def _memkv(mem, g_mem, w_xk, w_xv):
    blk = pl.BlockSpec((1, MEM_LEN, D_MODEL), lambda b: (b, 0, 0))
    full = lambda a: pl.BlockSpec(a.shape, lambda b: (0,) * a.ndim)
    return pl.pallas_call(
        _memkv_kernel,
        grid=(BATCH,),
        in_specs=[blk, full(g_mem), full(w_xk), full(w_xv)],
        out_specs=[blk, blk],
        out_shape=[jax.ShapeDtypeStruct((BATCH, MEM_LEN, D_MODEL), BF16)] * 2,
        compiler_params=_params(1, 32),
        name="memkv",
    )(mem, g_mem, w_xk, w_xv)


POST_TM = 512


def _post_kernel(x_ref, hm_ref, ha_ref, wo_ref, gc_ref, wq_ref, km_ref, vm_ref, wxo_ref, gf_ref,
                 h2_ref, xn3_ref):
    h1 = x_ref[...] + _dot(hm_ref[...], wo_ref[0:M_V_W, :]) + _dot(ha_ref[...], wo_ref[M_V_W:M_V_W + A_Q_W, :])
    xn2 = _rms(h1, gc_ref[...]).astype(BF16)
    q = _dot(xn2, wq_ref[...]).astype(BF16)
    heads = []
    for hd in range(X_HEADS):
        cols = slice(hd * X_DH, (hd + 1) * X_DH)
        sc = _dot_nt(q[:, cols], km_ref[0, :, cols])
        mx = jnp.max(sc, axis=-1, keepdims=True)
        pe = jnp.exp(sc - mx)
        den = jnp.sum(pe, axis=-1, keepdims=True)
        heads.append((_dot(pe.astype(BF16), vm_ref[0, :, cols]) / den).astype(BF16))
    h2 = h1 + _dot(jnp.concatenate(heads, axis=1), wxo_ref[...])
    h2_ref[...] = h2
    xn3_ref[...] = _rms(h2, gf_ref[...]).astype(BF16)


def _post(x2, hm2, ha2, w_out, g_cross, w_xq, kmem, vmem, w_xo, g_ffn):
    n = x2.shape[0]
    per_batch = SEQ // POST_TM
    row = lambda c: pl.BlockSpec((POST_TM, c), lambda i: (i, 0))
    full = lambda a: pl.BlockSpec(a.shape, lambda i: (0,) * a.ndim)
    memblk = pl.BlockSpec((1, MEM_LEN, D_MODEL), lambda i: (i // per_batch, 0, 0))
    return pl.pallas_call(
        _post_kernel,
        grid=(n // POST_TM,),
        in_specs=[row(D_MODEL), row(M_V_W), row(A_Q_W), full(w_out), full(g_cross), full(w_xq),
                  memblk, memblk, full(w_xo), full(g_ffn)],
        out_specs=[row(D_MODEL), row(D_MODEL)],
        out_shape=[jax.ShapeDtypeStruct((n, D_MODEL), F32), jax.ShapeDtypeStruct((n, D_MODEL), BF16)],
        compiler_params=_params(1, 48),
        name="post",
    )(x2, hm2, ha2, w_out, g_cross, w_xq, kmem, vmem, w_xo, g_ffn)


RT_T = 512
RT_G = LANES


def _ce(vals, i, j):
    a, b = vals[i], vals[j]
    vals[i] = jnp.maximum(a, b)
    vals[j] = jnp.minimum(a, b)


def _bitonic_sort_desc(vals):
    n = len(vals)
    k = 2
    while k <= n:
        j = k // 2
        while j >= 1:
            for i in range(n):
                l = i ^ j
                if l > i:
                    if (i & k) == 0:
                        _ce(vals, i, l)
                    else:
                        _ce(vals, l, i)
            j //= 2
        k *= 2


def _bitonic_merge_desc(vals):
    n = len(vals)
    j = n // 2
    while j >= 1:
        for i in range(n):
            l = i ^ j
            if l > i:
                _ce(vals, i, l)
        j //= 2


def _merge_top16(xs, ys):
    m = [jnp.maximum(xs[a], ys[P_TOPK - 1 - a]) for a in range(P_TOPK)]
    _bitonic_merge_desc(m)
    return m


def _top16_over_keys(vals):
    vals = list(vals)
    _bitonic_sort_desc(vals)
    for sh in (4, 2, 1):
        rolled = [pltpu.roll(v, sh, axis=0) for v in vals]
        vals = _merge_top16(vals, rolled)
    return vals


def _split_bf16(x):
    hi = x.astype(BF16)
    return hi, (x - hi.astype(F32)).astype(BF16)


def _route_kernel(xn_ref, wq_ref, k1_ref, k2_ref, cnt_ref, e1_ref, r2_ref, e2_ref):
    q_t = _dot_nt(wq_ref[...], xn_ref[...])
    k_parts = [_split_bf16(k1_ref[...]), _split_bf16(k2_ref[...])]
    s1_all, s2_all = [], []
    for h in range(P_HEADS):
        for half, dst in enumerate((s1_all, s2_all)):
            r0 = (2 * h + half) * P_KEY_DIM
            q_hi, q_lo = _split_bf16(q_t[r0:r0 + P_KEY_DIM, :])
            k_hi, k_lo = k_parts[half]
            main = _dot(k_hi, jnp.concatenate([q_hi, q_lo], axis=1))
            dst.append(main[:, 0:RT_T] + main[:, RT_T:2 * RT_T] + _dot(k_lo, q_hi))
    for g in range(RT_T // RT_G):
        ls = slice(g * RT_G, (g + 1) * RT_G)
        _route_select([s[:, ls] for s in s1_all], [s[:, ls] for s in s2_all], cnt_ref, e1_ref, r2_ref, e2_ref, ls)


def _route_select(s1_all, s2_all, cnt_ref, e1_ref, r2_ref, e2_ref, ls):
    probs = [s for pair in zip(s1_all, s2_all) for s in pair]
    wide = [jnp.concatenate([s[kb * SUBLANES:(kb + 1) * SUBLANES, :] for s in probs], axis=1)
            for kb in range(P_NKEYS // SUBLANES)]
    tops = _top16_over_keys(wide)
    sub = lax.broadcasted_iota(jnp.int32, (SUBLANES, RT_G), 0)
    v1 = [None] * P_TOPK
    v2 = [None] * P_TOPK
    for a in range(P_TOPK):
        for h in range(P_HEADS):
            t1 = tops[a][:, (2 * h) * RT_G:(2 * h + 1) * RT_G]
            t2 = tops[a][:, (2 * h + 1) * RT_G:(2 * h + 2) * RT_G]
            v1[a] = t1 if h == 0 else jnp.where(sub == h, t1, v1[a])
            v2[a] = t2 if h == 0 else jnp.where(sub == h, t2, v2[a])
    pairs = [(a, b) for a in range(P_TOPK) for b in range(P_TOPK // (a + 1))]
    cand_of = {ab: v1[ab[0]] + v2[ab[1]] for ab in pairs}
    pad = jnp.full((SUBLANES, RT_G), NEG_INF, F32)
    cands = [cand_of[ab] for ab in pairs] + [pad] * (64 - len(pairs))
    groups = []
    for gi in range(4):
        grp = cands[gi::4]
        _bitonic_sort_desc(grp)
        groups.append(grp)
    top = _merge_top16(_merge_top16(groups[0], groups[1]), _merge_top16(groups[2], groups[3]))
    tau = top[P_TOPK - 1]
    z = jnp.ones_like(tau)
    for kk in range(1, P_TOPK):
        z = z + jnp.exp(top[kk] - top[0])
    zinv = 1.0 / z
    cnt_rank = []
    for a in range(P_TOPK):
        c_a = jnp.zeros((SUBLANES, RT_G), F32)
        for b in range(P_TOPK // (a + 1)):
            c_a = jnp.where(cand_of[(a, b)] >= tau, float(b + 1), c_a)
        cnt_rank.append(c_a)
    for h in range(P_HEADS):
        s1, s2 = s1_all[h], s2_all[h]
        cnt_rows = jnp.zeros((P_NKEYS, RT_G), F32)
        for a in range(P_TOPK):
            cnt_rows = jnp.where(s1 == v1[a][h:h + 1, :], cnt_rank[a][h:h + 1, :], cnt_rows)
        rank2 = jnp.zeros((P_NKEYS, RT_G), F32)
        for b in range(P_TOPK):
            rank2 = jnp.where(s2 < v2[b][h:h + 1, :], float(b + 1), rank2)
        cnt_ref[h, :, ls] = cnt_rows
        e1_ref[h, :, ls] = jnp.exp(s1 - v1[0][h:h + 1, :]) * zinv[h:h + 1, :]
        r2_ref[h, :, ls] = pltpu.bitcast(rank2.astype(BF16), jnp.uint32)
        e2_ref[h, :, ls] = pltpu.bitcast(jnp.exp(s2 - v2[0][h:h + 1, :]).astype(BF16), jnp.uint32)


def _route(xn3, w_pq_t, keys1, keys2):
    n = xn3.shape[0]
    full = lambda a: pl.BlockSpec(a.shape, lambda i: (0,) * a.ndim)
    oblk = lambda rows: pl.BlockSpec((P_HEADS, rows, RT_T), lambda i: (0, 0, i))
    f32_rows = jax.ShapeDtypeStruct((P_HEADS, P_NKEYS, n), F32)
    packed_rows = jax.ShapeDtypeStruct((P_HEADS, P_NKEYS // 2, n), jnp.uint32)
    return pl.pallas_call(
        _route_kernel,
        grid=(n // RT_T,),
        in_specs=[pl.BlockSpec((RT_T, D_MODEL), lambda i: (i, 0)), full(w_pq_t), full(keys1), full(keys2)],
        out_specs=[oblk(P_NKEYS), oblk(P_NKEYS), oblk(P_NKEYS // 2), oblk(P_NKEYS // 2)],
        out_shape=[f32_rows, f32_rows, packed_rows, packed_rows],
        compiler_params=_params(1, 40),
        name="route",
    )(xn3, w_pq_t, keys1, keys2)


PE_T = 1024
PE_E = 1024
PE_NE = P_EXPERTS // PE_E
PE_ROWS = PE_E // P_NKEYS
_SQRT_HALF = 0.7071067811865476


def _peer_kernel(dn_ref, up_ref, xn_ref, cnt_ref, e1_ref, r2_ref, e2_ref, o_ref, a_buf, g_buf):
    s = pl.program_id(0)
    slot = lax.rem(s, 2)
    other = 1 - slot

    @pl.when(s == 0)
    def _():
        a_buf[1] = jnp.zeros((PE_E, PE_T), BF16)
        g_buf[0] = jnp.zeros((PE_E, PE_T), BF16)

    @pl.when((s == 0) | ((s >= 2) & (lax.rem(s - 2, PE_NE) == 0)))
    def _():
        o_ref[...] = jnp.zeros_like(o_ref)

    a_buf[slot] = _dot_nt(dn_ref[...], xn_ref[...]).astype(BF16)
    o_ref[...] += _dot(up_ref[...], g_buf[slot])

    n_slab = P_NKEYS // BF16_ROWS
    zero = jnp.zeros((), BF16)
    for c in range(PE_T // LANES):
        cs = slice(c * LANES, (c + 1) * LANES)
        for i in range(PE_ROWS):
            coef = [jnp.zeros((BF16_ROWS, LANES), BF16)] * n_slab
            for h in range(P_HEADS):
                cnt_t = jnp.broadcast_to(cnt_ref[h, 0, i:i + 1, cs], (BF16_ROWS, LANES)).astype(BF16)
                e1_t = jnp.broadcast_to(e1_ref[h, 0, i:i + 1, cs], (BF16_ROWS, LANES)).astype(BF16)
                for k in range(n_slab):
                    js = slice(k * SUBLANES, (k + 1) * SUBLANES)
                    r2 = pltpu.bitcast(r2_ref[h, js, cs], BF16)
                    e2 = pltpu.bitcast(e2_ref[h, js, cs], BF16)
                    coef[k] = coef[k] + jnp.where(r2 < cnt_t, e2, zero) * e1_t
            for k in range(n_slab):
                rows = slice(i * P_NKEYS + k * BF16_ROWS, i * P_NKEYS + (k + 1) * BF16_ROWS)
                a = a_buf[other, rows, cs]
                gl = 0.5 * a * (1.0 + lax.erf(a * _SQRT_HALF))
                g_buf[other, rows, cs] = gl * coef[k]


def _peer(dn, up_t, xn3, cnt, e1, r2, e2):
    n = xn3.shape[0]
    assert PE_ROWS % SUBLANES == 0
    n_pairs = (n // PE_T) * PE_NE
    pair = lambda s, lag: jnp.clip(s - lag, 0, n_pairs - 1)
    tok = lambda s, lag: pair(s, lag) // PE_NE
    tile = lambda s, lag: pair(s, lag) % PE_NE
    cnt, e1 = (a.reshape(P_HEADS, PE_NE, PE_ROWS, n) for a in (cnt, e1))
    rblk = pl.BlockSpec((P_HEADS, 1, PE_ROWS, PE_T), lambda s: (0, tile(s, 1), 0, tok(s, 1)))
    pblk = pl.BlockSpec((P_HEADS, P_NKEYS // 2, PE_T), lambda s: (0, 0, tok(s, 1)))
    return pl.pallas_call(
        _peer_kernel,
        grid=(n_pairs + 2,),
        in_specs=[pl.BlockSpec((PE_E, D_MODEL), lambda s: (tile(s, 0), 0)),
                  pl.BlockSpec((D_MODEL, PE_E), lambda s: (0, tile(s, 2))),
                  pl.BlockSpec((PE_T, D_MODEL), lambda s: (tok(s, 0), 0)),
                  rblk, rblk, pblk, pblk],
        out_specs=pl.BlockSpec((D_MODEL, PE_T), lambda s: (0, tok(s, 2))),
        out_shape=jax.ShapeDtypeStruct((D_MODEL, n), F32),
        scratch_shapes=[pltpu.VMEM((2, PE_E, PE_T), BF16), pltpu.VMEM((2, PE_E, PE_T), BF16)],
        compiler_params=_params(1, 56),
        name="peer",
    )(dn, up_t, xn3, cnt, e1, r2, e2)


FIN_TM = 512


def _final_kernel(h_ref, pt_ref, g_ref, o_ref):
    o_ref[...] = _rms(h_ref[...] + pt_ref[...].T, g_ref[...])


def _final(h2, peer_t, g_final):
    n = h2.shape[0]
    return pl.pallas_call(
        _final_kernel,
        grid=(n // FIN_TM,),
        in_specs=[pl.BlockSpec((FIN_TM, D_MODEL), lambda i: (i, 0)),
                  pl.BlockSpec((D_MODEL, FIN_TM), lambda i: (0, i)),
                  pl.BlockSpec(g_final.shape, lambda i: (0, 0))],
        out_specs=pl.BlockSpec((FIN_TM, D_MODEL), lambda i: (i, 0)),
        out_shape=jax.ShapeDtypeStruct((n, D_MODEL), F32),
        compiler_params=_params(1, 32),
        name="final",
    )(h2, peer_t, g_final)


def _pack_w_in(w):
    o = 0
    segs = {}
    for name, sz in (("mq", M_QK_W // 2), ("mk", M_QK_W // 2), ("mv", M_V_W), ("mo", M_V_W),
                     ("mi", M_HEADS), ("mf", M_HEADS), ("aq", A_Q_W), ("ak", A_KV_HEADS * A_DH),
                     ("av", A_KV_HEADS * A_DH)):
        segs[name] = w[:, o:o + sz]
        o += sz
    k0, k1 = segs["ak"][:, :A_DH], segs["ak"][:, A_DH:]
    v0, v1 = segs["av"][:, :A_DH], segs["av"][:, A_DH:]
    gates = jnp.concatenate([segs["mi"], segs["mf"], jnp.zeros((w.shape[0], LANES - 2 * M_HEADS), w.dtype)], axis=1)
    packed = jnp.concatenate([segs["mq"], segs["mk"], segs["aq"] * (A_DH ** -0.5), k0, k0, k1, k1, gates], axis=1)
    w_t = jnp.concatenate([segs["mv"], segs["mo"], v0, v0, v1, v1], axis=1).T
    return packed.astype(BF16), w_t.astype(BF16)


def kernel(x, mem, g_mix, w_in, conv_w, conv_b, b_igate, b_fgate, g_mhead, sinks, w_out, g_cross, g_mem,
           w_xq, w_xk, w_xv, w_xo, g_ffn, w_pq, sub_keys1, sub_keys2, expert_down, expert_up, g_final):
    assert x.shape == (BATCH, SEQ, D_MODEL) and w_in.shape[0] == 1
    l = 0
    row = lambda v: v.reshape(1, -1).astype(F32)
    x2 = x.reshape(NTOK, D_MODEL)

    qk_pre, aq, ak, gates, mv_t, mo_t, av_t = _mix_in(x2, row(g_mix[l]), *_pack_w_in(w_in[l]))

    gate_bias = jnp.concatenate([b_igate[l], b_fgate[l], jnp.zeros((LANES - 2 * M_HEADS,), F32)]).reshape(1, LANES)
    g_mhead_b = jnp.broadcast_to(g_mhead[l].astype(F32)[:, None], (M_V_W, LANES))
    r3 = lambda a: a.reshape(BATCH, SEQ, a.shape[-1])
    hm = _mlstm(r3(qk_pre), mv_t, mo_t, r3(gates), conv_w[l], row(conv_b[l]), gate_bias, g_mhead_b)

    sinks_b = jnp.broadcast_to(sinks[l].astype(F32)[:, None], (A_HEADS, LANES))
    ha = _swa(r3(aq), r3(ak), av_t, sinks_b)

    kmem, vmem = _memkv(mem, row(g_mem[l]), w_xk[l].astype(BF16), w_xv[l].astype(BF16))

    h2, xn3 = _post(x2, hm.reshape(NTOK, M_V_W), ha.reshape(NTOK, A_Q_W), w_out[l].astype(BF16), row(g_cross[l]),
                    (w_xq[l] * (X_DH ** -0.5)).astype(BF16), kmem, vmem, w_xo[l].astype(BF16), row(g_ffn[l]))

    cnt, e1, r2, e2 = _route(xn3, w_pq[l].T.astype(BF16), sub_keys1[l], sub_keys2[l])

    peer_t = _peer(expert_down[l].astype(BF16), expert_up[l].T.astype(BF16), xn3, cnt, e1, r2, e2)

    out = _final(h2, peer_t, row(g_final))
    return out.reshape(BATCH, SEQ, D_MODEL)
```

```python
import jax
import jax.numpy as jnp
from jax import lax
from jax.experimental import pallas as pl
from jax.experimental.pallas import tpu as pltpu

F32 = jnp.float32
BF16 = jnp.bfloat16

D_MODEL = 1024
BATCH = 2
SEQ = 8192
NTOK = BATCH * SEQ
MEM_LEN = 256
EPS = 1e-6

M_HEADS = 4
M_DV = 128
M_DQK = 64
M_CONV = 4
M_QK_W = 2 * M_HEADS * M_DQK
M_V_W = M_HEADS * M_DV

A_HEADS = 8
A_KV_HEADS = 2
A_DH = 64
WINDOW = 128
A_BLOCK = 128
A_Q_W = A_HEADS * A_DH
A_KV_W = 2 * A_KV_HEADS * A_DH

X_HEADS = 4
X_DH = 256

P_HEADS = 8
P_NKEYS = 128
P_EXPERTS = P_NKEYS * P_NKEYS
P_KEY_DIM = 128
P_TOPK = 16

LANES = 128
HALF = LANES // 2
SUBLANES = 8
BF16_ROWS = 2 * SUBLANES

NEG_INF = float("-inf")

V7X_VMEM_BYTES = 64 * 1024 * 1024
MIB = 1024 * 1024


def _params(ndims, vmem_mib):
    assert vmem_mib * MIB < V7X_VMEM_BYTES
    return pltpu.CompilerParams(dimension_semantics=("arbitrary",) * ndims, vmem_limit_bytes=vmem_mib * MIB)


def _rms(xf, g):
    return xf * lax.rsqrt(jnp.mean(xf * xf, axis=-1, keepdims=True) + EPS) * g


def _dot_nt(a, b):
    return lax.dot_general(a, b, (((1,), (1,)), ((), ())), preferred_element_type=F32)


def _dot(a, b):
    return jnp.dot(a, b, preferred_element_type=F32)


MIX_TM = 512
MIX_COLS = (0, M_QK_W, M_QK_W + A_Q_W, M_QK_W + A_Q_W + A_KV_W, M_QK_W + A_Q_W + A_KV_W + LANES)
MIX_ROWS = (0, M_V_W, 2 * M_V_W, 2 * M_V_W + A_KV_W)


def _mix_in_kernel(x_ref, g_ref, w_ref, wt_ref, qk_ref, aq_ref, ak_ref, gt_ref, mvt_ref, mot_ref, avt_ref):
    xn = _rms(x_ref[...], g_ref[...]).astype(BF16)
    c, r = MIX_COLS, MIX_ROWS
    qk_ref[...] = _dot(xn, w_ref[:, c[0]:c[1]])
    aq_ref[...] = _dot(xn, w_ref[:, c[1]:c[2]]).astype(BF16)
    ak_ref[...] = _dot(xn, w_ref[:, c[2]:c[3]]).astype(BF16)
    gt_ref[...] = _dot(xn, w_ref[:, c[3]:c[4]])
    mvt_ref[...] = _dot_nt(wt_ref[r[0]:r[1], :], xn).astype(BF16)
    mot_ref[...] = _dot_nt(wt_ref[r[1]:r[2], :], xn)
    avt_ref[...] = _dot_nt(wt_ref[r[2]:r[3], :], xn).astype(BF16)


def _mix_in(x2, g_mix, w_all, w_t):
    n = x2.shape[0]
    row = lambda c: pl.BlockSpec((MIX_TM, c), lambda i: (i, 0))
    col = lambda r: pl.BlockSpec((r, MIX_TM), lambda i: (0, i))
    full = lambda a: pl.BlockSpec(a.shape, lambda i: (0,) * a.ndim)
    return pl.pallas_call(
        _mix_in_kernel,
        grid=(n // MIX_TM,),
        in_specs=[row(D_MODEL), full(g_mix), full(w_all), full(w_t)],
        out_specs=[row(M_QK_W), row(A_Q_W), row(A_KV_W), row(LANES), col(M_V_W), col(M_V_W), col(A_KV_W)],
        out_shape=[
            jax.ShapeDtypeStruct((n, M_QK_W), F32),
            jax.ShapeDtypeStruct((n, A_Q_W), BF16),
            jax.ShapeDtypeStruct((n, A_KV_W), BF16),
            jax.ShapeDtypeStruct((n, LANES), F32),
            jax.ShapeDtypeStruct((M_V_W, n), BF16),
            jax.ShapeDtypeStruct((M_V_W, n), F32),
            jax.ShapeDtypeStruct((A_KV_W, n), BF16),
        ],
        compiler_params=_params(1, 40),
        name="mix_in",
    )(x2, g_mix, w_all, w_t)


ML_T = 128
ML_SUB = 2
ML_STEP = ML_SUB * ML_T
ML_HALO = 8


def _log_sigmoid(x):
    return jnp.minimum(x, 0.0) - jnp.log1p(jnp.exp(-jnp.abs(x)))


def _mlstm_kernel(qk_ref, halo_ref, mvt0_ref, mvt1_ref, mot0_ref, mot1_ref, gt_ref, cw_ref, cb_ref, gb_ref,
                  gmh_ref, hm_ref, xcat_ref, c_ref, n_ref, m_ref):
    step = pl.program_id(0)
    mvt_refs = (mvt0_ref, mvt1_ref)
    mot_refs = (mot0_ref, mot1_ref)

    @pl.when(step == 0)
    def _():
        c_ref[...] = jnp.zeros_like(c_ref)
        n_ref[...] = jnp.zeros_like(n_ref)
        m_ref[...] = jnp.zeros_like(m_ref)

    src = lax.broadcasted_iota(jnp.int32, (ML_T, ML_T), 0)
    dst = lax.broadcasted_iota(jnp.int32, (ML_T, ML_T), 1)
    lo = dst < HALF
    lo_row = lax.broadcasted_iota(jnp.int32, (1, LANES), 1) < HALF
    causal = src <= dst
    tri = (src >= dst).astype(F32)
    not_first = (step > 0).astype(F32)

    for b in range(BATCH):
        mvt_ref, mot_ref = mvt_refs[b], mot_refs[b]
        xcat_ref[b, 0:ML_HALO, :] = halo_ref[b] * not_first
        xcat_ref[b, ML_HALO:ML_HALO + ML_STEP, :] = qk_ref[b]
        acc = jnp.broadcast_to(cb_ref[...], (ML_STEP, M_QK_W))
        for j in range(M_CONV):
            acc = acc + cw_ref[j:j + 1, :] * xcat_ref[b, pl.ds(ML_HALO - (M_CONV - 1) + j, ML_STEP), :]
        qk_step = acc * jax.nn.sigmoid(acc)

        gb_step = gt_ref[b] + gb_ref[...]
        lf_step = _log_sigmoid(gb_step)
        chunks = [slice(u * ML_T, (u + 1) * ML_T) for u in range(ML_SUB)]
        bcs = [jnp.dot(tri, lf_step[tok], precision=lax.Precision.HIGHEST, preferred_element_type=F32)
               for tok in chunks]
        b_ts = [bc.T for bc in bcs]
        for u, p in [(u, p) for u in range(ML_SUB) for p in range(M_HEADS // 2)]:
            tok, bc, b_t = chunks[u], bcs[u], b_ts[u]
            qk, gb = qk_step[tok], gb_step[tok]
            qp = qk[:, p * LANES:(p + 1) * LANES] * (M_DQK ** -0.5)
            kp = qk[:, M_QK_W // 2 + p * LANES:M_QK_W // 2 + (p + 1) * LANES]
            kp_bf = kp.astype(BF16)
            nrow = n_ref[b * 2 + p][0:1, :]
            n_bf = jnp.broadcast_to(nrow, (SUBLANES, LANES)).astype(BF16)
            n_parts = []
            for hh in range(2):
                h = 2 * p + hh
                sidx = b * M_HEADS + h
                mh = lo if hh == 0 else jnp.logical_not(lo)
                qm_bf = jnp.where(mh, qp, 0.0).astype(BF16)
                s_t = _dot_nt(kp_bf, qm_bf)
                m_prev = m_ref[sidx][0:1, 0:1]
                b_row = b_t[M_HEADS + h:M_HEADS + h + 1, :]
                col = jnp.broadcast_to(gb[:, h:h + 1] - bc[:, M_HEADS + h:M_HEADS + h + 1], (ML_T, ML_T))
                log_d = jnp.where(causal, b_row + col, NEG_INF)
                inter = b_row + m_prev
                m_t = jnp.maximum(inter, jnp.max(log_d, axis=0, keepdims=True))
                sp = s_t * jnp.exp(log_d - m_t)
                w_inter = jnp.exp(inter - m_t)
                qn = _dot_nt(n_bf, qm_bf)[0:1, :]
                den = jnp.sum(sp, axis=0, keepdims=True) + w_inter * qn
                v_t = mvt_ref[h * M_DV:(h + 1) * M_DV, tok]
                c_h = c_ref[sidx]
                num_t = _dot(v_t, sp.astype(BF16)) + w_inter * _dot_nt(c_h.astype(BF16), qm_bf)
                h_t = num_t / jnp.maximum(jnp.abs(den), jnp.exp(-m_t))
                h_n = h_t * lax.rsqrt(jnp.mean(h_t * h_t, axis=0, keepdims=True) + EPS)
                rows_h = slice(h * M_DV, (h + 1) * M_DV)
                out_t = jax.nn.sigmoid(mot_ref[rows_h, tok]) * (h_n * gmh_ref[rows_h, :])
                hm_ref[b, tok, h * M_DV:(h + 1) * M_DV] = out_t.T.astype(BF16)
                m_new = m_t[:, ML_T - 1:ML_T]
                b_last = b_row[:, ML_T - 1:ML_T]
                decay = jnp.exp(b_last + m_prev - m_new)
                wk = jnp.exp(col + (b_last - m_new)) * kp
                c_ref[sidx] = decay * c_h + jnp.where(mh, _dot(v_t, wk.astype(BF16)), 0.0)
                n_parts.append(decay * nrow + jnp.sum(wk, axis=0, keepdims=True))
                m_ref[sidx] = jnp.broadcast_to(m_new, (SUBLANES, LANES))
            n_new = jnp.where(lo_row, n_parts[0], n_parts[1])
            n_ref[b * 2 + p] = jnp.broadcast_to(n_new, (SUBLANES, LANES))


def _mlstm(qk_pre, mv_t, mo_t, gates, conv_w, conv_b, gate_bias, g_mhead_b):
    nblk = SEQ // ML_STEP
    blk = lambda c: pl.BlockSpec((BATCH, ML_STEP, c), lambda i: (0, i, 0))
    full = lambda a: pl.BlockSpec(a.shape, lambda i: (0,) * a.ndim)
    halo = pl.BlockSpec((BATCH, ML_HALO, M_QK_W),
                        lambda i: (0, jnp.maximum(i * (ML_STEP // ML_HALO) - 1, 0), 0))
    t_blk = lambda b: pl.BlockSpec((M_V_W, ML_STEP), lambda i: (0, b * nblk + i))
    return pl.pallas_call(
        _mlstm_kernel,
        grid=(nblk,),
        in_specs=[blk(M_QK_W), halo, t_blk(0), t_blk(1), t_blk(0), t_blk(1), blk(LANES),
                  full(conv_w), full(conv_b), full(gate_bias), full(g_mhead_b)],
        out_specs=blk(M_V_W),
        out_shape=jax.ShapeDtypeStruct((BATCH, SEQ, M_V_W), BF16),
        scratch_shapes=[
            pltpu.VMEM((BATCH, ML_HALO + ML_STEP, M_QK_W), F32),
            pltpu.VMEM((BATCH * M_HEADS, M_DV, LANES), F32),
            pltpu.VMEM((BATCH * 2, SUBLANES, LANES), F32),
            pltpu.VMEM((BATCH * M_HEADS, SUBLANES, LANES), F32),
        ],
        compiler_params=_params(1, 32),
        name="mlstm",
    )(qk_pre, qk_pre, mv_t, mv_t, mo_t, mo_t, gates, conv_w, conv_b, gate_bias, g_mhead_b)


SWA_SUB = 8
SWA_T = SWA_SUB * A_BLOCK


def _swa_kernel(q_ref, kp_ref, kc_ref, vtp_ref, vtc_ref, sk_ref, o_ref):
    n = pl.program_id(1)
    lo = lax.broadcasted_iota(jnp.int32, (A_BLOCK, LANES), 1) < HALF
    top = lax.broadcasted_iota(jnp.int32, (LANES, A_BLOCK), 0) < HALF
    kj = lax.broadcasted_iota(jnp.int32, (2 * A_BLOCK, A_BLOCK), 0)
    qi = lax.broadcasted_iota(jnp.int32, (2 * A_BLOCK, A_BLOCK), 1)
    diff = qi - kj + A_BLOCK
    band = (diff >= 0) & (diff < WINDOW)
    band_first = band & ((kj >= A_BLOCK) | (n > 0))
    for sb in range(SWA_SUB):
        q_rows = slice(sb * A_BLOCK, (sb + 1) * A_BLOCK)
        mask = band_first if sb == 0 else band
        for j in range(A_KV_HEADS):
            cols = slice(j * LANES, (j + 1) * LANES)
            if sb == 0:
                kk = jnp.concatenate([kp_ref[0, :, cols], kc_ref[0, 0:A_BLOCK, cols]], axis=0)
                vt = jnp.concatenate([vtp_ref[cols, :], vtc_ref[cols, 0:A_BLOCK]], axis=1)
            else:
                kv_rows = slice((sb - 1) * A_BLOCK, (sb + 1) * A_BLOCK)
                kk = kc_ref[0, kv_rows, cols]
                vt = vtc_ref[cols, kv_rows]
            for p in range(2):
                c0 = (2 * j + p) * LANES
                qp = q_ref[0, q_rows, c0:c0 + LANES].astype(F32)
                outs = []
                for hh in range(2):
                    h = 4 * j + 2 * p + hh
                    mh = lo if hh == 0 else jnp.logical_not(lo)
                    qm = jnp.where(mh, qp, 0.0).astype(BF16)
                    sc = jnp.where(mask, _dot_nt(kk, qm), NEG_INF)
                    sink = sk_ref[h:h + 1, 0:1]
                    mx = jnp.maximum(jnp.max(sc, axis=0, keepdims=True), sink)
                    pe = jnp.exp(sc - mx)
                    den = jnp.sum(pe, axis=0, keepdims=True) + jnp.exp(sink - mx)
                    outs.append(_dot(vt, pe.astype(BF16)) / den)
                o_t = jnp.where(top, outs[0], outs[1])
                o_ref[0, q_rows, c0:c0 + LANES] = o_t.T.astype(BF16)


def _swa(aq, ak, av_t, sinks_b):
    per_b = SEQ // SWA_T
    cur = lambda c: pl.BlockSpec((1, SWA_T, c), lambda b, n: (b, n, 0))
    prev = lambda c: pl.BlockSpec((1, A_BLOCK, c), lambda b, n: (b, jnp.maximum(n * SWA_SUB - 1, 0), 0))
    vt_cur = pl.BlockSpec((A_KV_W, SWA_T), lambda b, n: (0, b * per_b + n))
    vt_prev = pl.BlockSpec((A_KV_W, A_BLOCK),
                           lambda b, n: (0, b * per_b * SWA_SUB + jnp.maximum(n * SWA_SUB - 1, 0)))
    return pl.pallas_call(
        _swa_kernel,
        grid=(BATCH, per_b),
        in_specs=[cur(A_Q_W), prev(A_KV_W), cur(A_KV_W), vt_prev, vt_cur,
                  pl.BlockSpec(sinks_b.shape, lambda b, n: (0, 0))],
        out_specs=cur(A_Q_W),
        out_shape=jax.ShapeDtypeStruct((BATCH, SEQ, A_Q_W), BF16),
        compiler_params=_params(2, 32),
        name="swa",
    )(aq, ak, ak, av_t, av_t, sinks_b)


def _memkv_kernel(mem_ref, g_ref, wk_ref, wv_ref, k_ref, v_ref):
    mn = _rms(mem_ref[0], g_ref[...]).astype(BF16)
    k_ref[0] = _dot(mn, wk_ref[...]).astype(BF16)
    v_ref[0] = _dot(mn, wv_ref[...]).astype(BF16)


def _memkv(mem, g_mem, w_xk, w_xv):
    blk = pl.BlockSpec((1, MEM_LEN, D_MODEL), lambda b: (b, 0, 0))
    full = lambda a: pl.BlockSpec(a.shape, lambda b: (0,) * a.ndim)
    return pl.pallas_call(
        _memkv_kernel,
        grid=(BATCH,),
        in_specs=[blk, full(g_mem), full(w_xk), full(w_xv)],
        out_specs=[blk, blk],
        out_shape=[jax.ShapeDtypeStruct((BATCH, MEM_LEN, D_MODEL), BF16)] * 2,
        compiler_params=_params(1, 32),
        name="memkv",
    )(mem, g_mem, w_xk, w_xv)


POST_TM = 512


def _post_kernel(x_ref, hm_ref, ha_ref, wo_ref, gc_ref, wq_ref, km_ref, vm_ref, wxo_ref, gf_ref,
                 h2_ref, xn3_ref):
    h1 = x_ref[...] + _dot(hm_ref[...], wo_ref[0:M_V_W, :]) + _dot(ha_ref[...], wo_ref[M_V_W:M_V_W + A_Q_W, :])
    xn2 = _rms(h1, gc_ref[...]).astype(BF16)
    q = _dot(xn2, wq_ref[...]).astype(BF16)
    heads = []
    for hd in range(X_HEADS):
        cols = slice(hd * X_DH, (hd + 1) * X_DH)
        sc = _dot_nt(q[:, cols], km_ref[0, :, cols])
        mx = jnp.max(sc, axis=-1, keepdims=True)
        pe = jnp.exp(sc - mx)
        den = jnp.sum(pe, axis=-1, keepdims=True)
        heads.append((_dot(pe.astype(BF16), vm_ref[0, :, cols]) / den).astype(BF16))
    h2 = h1 + _dot(jnp.concatenate(heads, axis=1), wxo_ref[...])
    h2_ref[...] = h2
    xn3_ref[...] = _rms(h2, gf_ref[...]).astype(BF16)


def _post(x2, hm2, ha2, w_out, g_cross, w_xq, kmem, vmem, w_xo, g_ffn):
    n = x2.shape[0]
    per_batch = SEQ // POST_TM
    row = lambda c: pl.BlockSpec((POST_TM, c), lambda i: (i, 0))
    full = lambda a: pl.BlockSpec(a.shape, lambda i: (0,) * a.ndim)
    memblk = pl.BlockSpec((1, MEM_LEN, D_MODEL), lambda i: (i // per_batch, 0, 0))
    return pl.pallas_call(
        _post_kernel,
        grid=(n // POST_TM,),
        in_specs=[row(D_MODEL), row(M_V_W), row(A_Q_W), full(w_out), full(g_cross), full(w_xq),
                  memblk, memblk, full(w_xo), full(g_ffn)],
        out_specs=[row(D_MODEL), row(D_MODEL)],
        out_shape=[jax.ShapeDtypeStruct((n, D_MODEL), F32), jax.ShapeDtypeStruct((n, D_MODEL), BF16)],
        compiler_params=_params(1, 48),
        name="post",
    )(x2, hm2, ha2, w_out, g_cross, w_xq, kmem, vmem, w_xo, g_ffn)


RT_T = 512
RT_G = LANES


def _ce(vals, i, j):
    a, b = vals[i], vals[j]
    vals[i] = jnp.maximum(a, b)
    vals[j] = jnp.minimum(a, b)


def _bitonic_sort_desc(vals):
    n = len(vals)
    k = 2
    while k <= n:
        j = k // 2
        while j >= 1:
            for i in range(n):
                l = i ^ j
                if l > i:
                    if (i & k) == 0:
                        _ce(vals, i, l)
                    else:
                        _ce(vals, l, i)
            j //= 2
        k *= 2


def _bitonic_merge_desc(vals):
    n = len(vals)
    j = n // 2
    while j >= 1:
        for i in range(n):
            l = i ^ j
            if l > i:
                _ce(vals, i, l)
        j //= 2


def _merge_top16(xs, ys):
    m = [jnp.maximum(xs[a], ys[P_TOPK - 1 - a]) for a in range(P_TOPK)]
    _bitonic_merge_desc(m)
    return m


def _top16_over_keys(vals):
    vals = list(vals)
    _bitonic_sort_desc(vals)
    for sh in (4, 2, 1):
        rolled = [pltpu.roll(v, sh, axis=0) for v in vals]
        vals = _merge_top16(vals, rolled)
    return vals


def _split_bf16(x):
    hi = x.astype(BF16)
    return hi, (x - hi.astype(F32)).astype(BF16)


def _route_kernel(xn_ref, wq_ref, k1_ref, k2_ref, cnt_ref, e1_ref, r2_ref, e2_ref):
    q_t = _dot_nt(wq_ref[...], xn_ref[...])
    k_parts = [_split_bf16(k1_ref[...]), _split_bf16(k2_ref[...])]
    s1_all, s2_all = [], []
    for h in range(P_HEADS):
        for half, dst in enumerate((s1_all, s2_all)):
            r0 = (2 * h + half) * P_KEY_DIM
            q_hi, q_lo = _split_bf16(q_t[r0:r0 + P_KEY_DIM, :])
            k_hi, k_lo = k_parts[half]
            main = _dot(k_hi, jnp.concatenate([q_hi, q_lo], axis=1))
            dst.append(main[:, 0:RT_T] + main[:, RT_T:2 * RT_T] + _dot(k_lo, q_hi))
    for g in range(RT_T // RT_G):
        ls = slice(g * RT_G, (g + 1) * RT_G)
        _route_select([s[:, ls] for s in s1_all], [s[:, ls] for s in s2_all], cnt_ref, e1_ref, r2_ref, e2_ref, ls)


def _route_select(s1_all, s2_all, cnt_ref, e1_ref, r2_ref, e2_ref, ls):
    probs = [s for pair in zip(s1_all, s2_all) for s in pair]
    wide = [jnp.concatenate([s[kb * SUBLANES:(kb + 1) * SUBLANES, :] for s in probs], axis=1)
            for kb in range(P_NKEYS // SUBLANES)]
    tops = _top16_over_keys(wide)
    sub = lax.broadcasted_iota(jnp.int32, (SUBLANES, RT_G), 0)
    v1 = [None] * P_TOPK
    v2 = [None] * P_TOPK
    for a in range(P_TOPK):
        for h in range(P_HEADS):
            t1 = tops[a][:, (2 * h) * RT_G:(2 * h + 1) * RT_G]
            t2 = tops[a][:, (2 * h + 1) * RT_G:(2 * h + 2) * RT_G]
            v1[a] = t1 if h == 0 else jnp.where(sub == h, t1, v1[a])
            v2[a] = t2 if h == 0 else jnp.where(sub == h, t2, v2[a])
    pairs = [(a, b) for a in range(P_TOPK) for b in range(P_TOPK // (a + 1))]
    cand_of = {ab: v1[ab[0]] + v2[ab[1]] for ab in pairs}
    pad = jnp.full((SUBLANES, RT_G), NEG_INF, F32)
    cands = [cand_of[ab] for ab in pairs] + [pad] * (64 - len(pairs))
    groups = []
    for gi in range(4):
        grp = cands[gi::4]
        _bitonic_sort_desc(grp)
        groups.append(grp)
    top = _merge_top16(_merge_top16(groups[0], groups[1]), _merge_top16(groups[2], groups[3]))
    tau = top[P_TOPK - 1]
    z = jnp.ones_like(tau)
    for kk in range(1, P_TOPK):
        z = z + jnp.exp(top[kk] - top[0])
    zinv = 1.0 / z
    cnt_rank = []
    for a in range(P_TOPK):
        c_a = jnp.zeros((SUBLANES, RT_G), F32)
        for b in range(P_TOPK // (a + 1)):
            c_a = jnp.where(cand_of[(a, b)] >= tau, float(b + 1), c_a)
        cnt_rank.append(c_a)
    for h in range(P_HEADS):
        s1, s2 = s1_all[h], s2_all[h]
        cnt_rows = jnp.zeros((P_NKEYS, RT_G), F32)
        for a in range(P_TOPK):
            cnt_rows = jnp.where(s1 == v1[a][h:h + 1, :], cnt_rank[a][h:h + 1, :], cnt_rows)
        rank2 = jnp.zeros((P_NKEYS, RT_G), F32)
        for b in range(P_TOPK):
            rank2 = jnp.where(s2 < v2[b][h:h + 1, :], float(b + 1), rank2)
        cnt_ref[h, :, ls] = cnt_rows
        e1_ref[h, :, ls] = jnp.exp(s1 - v1[0][h:h + 1, :]) * zinv[h:h + 1, :]
        r2_ref[h, :, ls] = pltpu.bitcast(rank2.astype(BF16), jnp.uint32)
        e2_ref[h, :, ls] = pltpu.bitcast(jnp.exp(s2 - v2[0][h:h + 1, :]).astype(BF16), jnp.uint32)


def _route(xn3, w_pq_t, keys1, keys2):
    n = xn3.shape[0]
    full = lambda a: pl.BlockSpec(a.shape, lambda i: (0,) * a.ndim)
    oblk = lambda rows: pl.BlockSpec((P_HEADS, rows, RT_T), lambda i: (0, 0, i))
    f32_rows = jax.ShapeDtypeStruct((P_HEADS, P_NKEYS, n), F32)
    packed_rows = jax.ShapeDtypeStruct((P_HEADS, P_NKEYS // 2, n), jnp.uint32)
    return pl.pallas_call(
        _route_kernel,
        grid=(n // RT_T,),
        in_specs=[pl.BlockSpec((RT_T, D_MODEL), lambda i: (i, 0)), full(w_pq_t), full(keys1), full(keys2)],
        out_specs=[oblk(P_NKEYS), oblk(P_NKEYS), oblk(P_NKEYS // 2), oblk(P_NKEYS // 2)],
        out_shape=[f32_rows, f32_rows, packed_rows, packed_rows],
        compiler_params=_params(1, 40),
        name="route",
    )(xn3, w_pq_t, keys1, keys2)


PE_T = 1024
PE_E = 1024
PE_NE = P_EXPERTS // PE_E
PE_ROWS = PE_E // P_NKEYS
_SQRT_HALF = 0.7071067811865476


def _peer_kernel(dn_ref, up_ref, xn_ref, cnt_ref, e1_ref, r2_ref, e2_ref, h2_ref, gfin_ref, o_ref,
                 a_buf, g_buf, acc_ref):
    s = pl.program_id(0)
    slot = lax.rem(s, 2)
    other = 1 - slot
    up_tile = lax.rem(s - 2, PE_NE)

    @pl.when(s == 0)
    def _():
        a_buf[1] = jnp.zeros((PE_E, PE_T), BF16)
        g_buf[0] = jnp.zeros((PE_E, PE_T), BF16)

    @pl.when((s == 0) | ((s >= 2) & (up_tile == 0)))
    def _():
        acc_ref[...] = jnp.zeros_like(acc_ref)

    a_buf[slot] = _dot_nt(dn_ref[...], xn_ref[...]).astype(BF16)
    acc_ref[...] += _dot(up_ref[...], g_buf[slot])

    n_slab = P_NKEYS // BF16_ROWS
    zero = jnp.zeros((), BF16)
    for c in range(PE_T // LANES):
        cs = slice(c * LANES, (c + 1) * LANES)
        for i in range(PE_ROWS):
            coef = [jnp.zeros((BF16_ROWS, LANES), BF16)] * n_slab
            for h in range(P_HEADS):
                cnt_t = jnp.broadcast_to(cnt_ref[h, 0, i:i + 1, cs], (BF16_ROWS, LANES)).astype(BF16)
                e1_t = jnp.broadcast_to(e1_ref[h, 0, i:i + 1, cs], (BF16_ROWS, LANES)).astype(BF16)
                for k in range(n_slab):
                    js = slice(k * SUBLANES, (k + 1) * SUBLANES)
                    r2 = pltpu.bitcast(r2_ref[h, js, cs], BF16)
                    e2 = pltpu.bitcast(e2_ref[h, js, cs], BF16)
                    coef[k] = coef[k] + jnp.where(r2 < cnt_t, e2, zero) * e1_t
            for k in range(n_slab):
                rows = slice(i * P_NKEYS + k * BF16_ROWS, i * P_NKEYS + (k + 1) * BF16_ROWS)
                a = a_buf[other, rows, cs]
                gl = 0.5 * a * (1.0 + lax.erf(a * _SQRT_HALF))
                g_buf[other, rows, cs] = gl * coef[k]

    @pl.when((s >= 2) & (up_tile == PE_NE - 1))
    def _():
        o_ref[...] = _rms(h2_ref[...] + acc_ref[...].T, gfin_ref[...])


def _peer(dn, up_t, xn3, cnt, e1, r2, e2, h2, g_final):
    n = xn3.shape[0]
    assert PE_ROWS % SUBLANES == 0
    n_pairs = (n // PE_T) * PE_NE
    pair = lambda s, lag: jnp.clip(s - lag, 0, n_pairs - 1)
    tok = lambda s, lag: pair(s, lag) // PE_NE
    tile = lambda s, lag: pair(s, lag) % PE_NE
    cnt, e1 = (a.reshape(P_HEADS, PE_NE, PE_ROWS, n) for a in (cnt, e1))
    rblk = pl.BlockSpec((P_HEADS, 1, PE_ROWS, PE_T), lambda s: (0, tile(s, 1), 0, tok(s, 1)))
    pblk = pl.BlockSpec((P_HEADS, P_NKEYS // 2, PE_T), lambda s: (0, 0, tok(s, 1)))
    return pl.pallas_call(
        _peer_kernel,
        grid=(n_pairs + 2,),
        in_specs=[pl.BlockSpec((PE_E, D_MODEL), lambda s: (tile(s, 0), 0)),
                  pl.BlockSpec((D_MODEL, PE_E), lambda s: (0, tile(s, 2))),
                  pl.BlockSpec((PE_T, D_MODEL), lambda s: (tok(s, 0), 0)),
                  rblk, rblk, pblk, pblk,
                  pl.BlockSpec((PE_T, D_MODEL), lambda s: (tok(s, 2), 0)),
                  pl.BlockSpec(g_final.shape, lambda s: (0, 0))],
        out_specs=pl.BlockSpec((PE_T, D_MODEL), lambda s: (tok(s, 2), 0)),
        out_shape=jax.ShapeDtypeStruct((n, D_MODEL), F32),
        scratch_shapes=[pltpu.VMEM((2, PE_E, PE_T), BF16), pltpu.VMEM((2, PE_E, PE_T), BF16),
                        pltpu.VMEM((D_MODEL, PE_T), F32)],
        compiler_params=_params(1, 56),
        name="peer",
    )(dn, up_t, xn3, cnt, e1, r2, e2, h2, g_final)


def _pack_w_in(w):
    o = 0
    segs = {}
    for name, sz in (("mq", M_QK_W // 2), ("mk", M_QK_W // 2), ("mv", M_V_W), ("mo", M_V_W),
                     ("mi", M_HEADS), ("mf", M_HEADS), ("aq", A_Q_W), ("ak", A_KV_HEADS * A_DH),
                     ("av", A_KV_HEADS * A_DH)):
        segs[name] = w[:, o:o + sz]
        o += sz
    k0, k1 = segs["ak"][:, :A_DH], segs["ak"][:, A_DH:]
    v0, v1 = segs["av"][:, :A_DH], segs["av"][:, A_DH:]
    gates = jnp.concatenate([segs["mi"], segs["mf"], jnp.zeros((w.shape[0], LANES - 2 * M_HEADS), w.dtype)], axis=1)
    packed = jnp.concatenate([segs["mq"], segs["mk"], segs["aq"] * (A_DH ** -0.5), k0, k0, k1, k1, gates], axis=1)
    w_t = jnp.concatenate([segs["mv"], segs["mo"], v0, v0, v1, v1], axis=1).T
    return packed.astype(BF16), w_t.astype(BF16)


def kernel(x, mem, g_mix, w_in, conv_w, conv_b, b_igate, b_fgate, g_mhead, sinks, w_out, g_cross, g_mem,
           w_xq, w_xk, w_xv, w_xo, g_ffn, w_pq, sub_keys1, sub_keys2, expert_down, expert_up, g_final):
    assert x.shape == (BATCH, SEQ, D_MODEL) and w_in.shape[0] == 1
    l = 0
    row = lambda v: v.reshape(1, -1).astype(F32)
    x2 = x.reshape(NTOK, D_MODEL)

    qk_pre, aq, ak, gates, mv_t, mo_t, av_t = _mix_in(x2, row(g_mix[l]), *_pack_w_in(w_in[l]))

    gate_bias = jnp.concatenate([b_igate[l], b_fgate[l], jnp.zeros((LANES - 2 * M_HEADS,), F32)]).reshape(1, LANES)
    g_mhead_b = jnp.broadcast_to(g_mhead[l].astype(F32)[:, None], (M_V_W, LANES))
    r3 = lambda a: a.reshape(BATCH, SEQ, a.shape[-1])
    hm = _mlstm(r3(qk_pre), mv_t, mo_t, r3(gates), conv_w[l], row(conv_b[l]), gate_bias, g_mhead_b)

    sinks_b = jnp.broadcast_to(sinks[l].astype(F32)[:, None], (A_HEADS, LANES))
    ha = _swa(r3(aq), r3(ak), av_t, sinks_b)

    kmem, vmem = _memkv(mem, row(g_mem[l]), w_xk[l].astype(BF16), w_xv[l].astype(BF16))

    h2, xn3 = _post(x2, hm.reshape(NTOK, M_V_W), ha.reshape(NTOK, A_Q_W), w_out[l].astype(BF16), row(g_cross[l]),
                    (w_xq[l] * (X_DH ** -0.5)).astype(BF16), kmem, vmem, w_xo[l].astype(BF16), row(g_ffn[l]))

    cnt, e1, r2, e2 = _route(xn3, w_pq[l].T.astype(BF16), sub_keys1[l], sub_keys2[l])

    out = _peer(expert_down[l].astype(BF16), expert_up[l].T.astype(BF16), xn3, cnt, e1, r2, e2, h2, row(g_final))
    return out.reshape(BATCH, SEQ, D_MODEL)
```

```python
import jax
import jax.numpy as jnp
from jax import lax
from jax.experimental import pallas as pl
from jax.experimental.pallas import tpu as pltpu

F32 = jnp.float32
BF16 = jnp.bfloat16

D_MODEL = 1024
BATCH = 2
SEQ = 8192
NTOK = BATCH * SEQ
MEM_LEN = 256
EPS = 1e-6

M_HEADS = 4
M_DV = 128
M_DQK = 64
M_CONV = 4
M_QK_W = 2 * M_HEADS * M_DQK
M_V_W = M_HEADS * M_DV

A_HEADS = 8
A_KV_HEADS = 2
A_DH = 64
WINDOW = 128
A_BLOCK = 128
A_Q_W = A_HEADS * A_DH
A_KV_W = 2 * A_KV_HEADS * A_DH

X_HEADS = 4
X_DH = 256

P_HEADS = 8
P_NKEYS = 128
P_EXPERTS = P_NKEYS * P_NKEYS
P_KEY_DIM = 128
P_TOPK = 16

LANES = 128
HALF = LANES // 2
SUBLANES = 8
BF16_ROWS = 2 * SUBLANES

NEG_INF = float("-inf")

V7X_VMEM_BYTES = 64 * 1024 * 1024
MIB = 1024 * 1024


def _params(ndims, vmem_mib):
    assert vmem_mib * MIB < V7X_VMEM_BYTES
    return pltpu.CompilerParams(dimension_semantics=("arbitrary",) * ndims, vmem_limit_bytes=vmem_mib * MIB)


def _rms(xf, g):
    return xf * lax.rsqrt(jnp.mean(xf * xf, axis=-1, keepdims=True) + EPS) * g


def _dot_nt(a, b):
    return lax.dot_general(a, b, (((1,), (1,)), ((), ())), preferred_element_type=F32)


def _dot(a, b):
    return jnp.dot(a, b, preferred_element_type=F32)


MIX_TM = 512
MIX_COLS = (0, M_QK_W, M_QK_W + A_Q_W, M_QK_W + A_Q_W + A_KV_W, M_QK_W + A_Q_W + A_KV_W + LANES)
MIX_ROWS = (0, M_V_W, 2 * M_V_W, 2 * M_V_W + A_KV_W)


def _mix_in_kernel(x_ref, g_ref, w_ref, wt_ref, qk_ref, aq_ref, ak_ref, gt_ref, mvt_ref, mot_ref, avt_ref):
    xn = _rms(x_ref[...], g_ref[...]).astype(BF16)
    c, r = MIX_COLS, MIX_ROWS
    qk_ref[...] = _dot(xn, w_ref[:, c[0]:c[1]])
    aq_ref[...] = _dot(xn, w_ref[:, c[1]:c[2]]).astype(BF16)
    ak_ref[...] = _dot(xn, w_ref[:, c[2]:c[3]]).astype(BF16)
    gt_ref[...] = _dot(xn, w_ref[:, c[3]:c[4]])
    mvt_ref[...] = _dot_nt(wt_ref[r[0]:r[1], :], xn).astype(BF16)
    mot_ref[...] = _dot_nt(wt_ref[r[1]:r[2], :], xn)
    avt_ref[...] = _dot_nt(wt_ref[r[2]:r[3], :], xn).astype(BF16)


def _mix_in(x2, g_mix, w_all, w_t):
    n = x2.shape[0]
    row = lambda c: pl.BlockSpec((MIX_TM, c), lambda i: (i, 0))
    col = lambda r: pl.BlockSpec((r, MIX_TM), lambda i: (0, i))
    full = lambda a: pl.BlockSpec(a.shape, lambda i: (0,) * a.ndim)
    return pl.pallas_call(
        _mix_in_kernel,
        grid=(n // MIX_TM,),
        in_specs=[row(D_MODEL), full(g_mix), full(w_all), full(w_t)],
        out_specs=[row(M_QK_W), row(A_Q_W), row(A_KV_W), row(LANES), col(M_V_W), col(M_V_W), col(A_KV_W)],
        out_shape=[
            jax.ShapeDtypeStruct((n, M_QK_W), F32),
            jax.ShapeDtypeStruct((n, A_Q_W), BF16),
            jax.ShapeDtypeStruct((n, A_KV_W), BF16),
            jax.ShapeDtypeStruct((n, LANES), F32),
            jax.ShapeDtypeStruct((M_V_W, n), BF16),
            jax.ShapeDtypeStruct((M_V_W, n), F32),
            jax.ShapeDtypeStruct((A_KV_W, n), BF16),
        ],
        compiler_params=_params(1, 40),
        name="mix_in",
    )(x2, g_mix, w_all, w_t)


ML_T = 128
ML_SUB = 4
ML_STEP = ML_SUB * ML_T
ML_HALO = 8


def _log_sigmoid(x):
    return jnp.minimum(x, 0.0) - jnp.log1p(jnp.exp(-jnp.abs(x)))


def _mlstm_kernel(qk_ref, halo_ref, mvt0_ref, mvt1_ref, mot0_ref, mot1_ref, gt_ref, cw_ref, cb_ref, gb_ref,
                  gmh_ref, hm_ref, xcat_ref, c_ref, n_ref, m_ref):
    step = pl.program_id(0)
    mvt_refs = (mvt0_ref, mvt1_ref)
    mot_refs = (mot0_ref, mot1_ref)

    @pl.when(step == 0)
    def _():
        c_ref[...] = jnp.zeros_like(c_ref)
        n_ref[...] = jnp.zeros_like(n_ref)
        m_ref[...] = jnp.zeros_like(m_ref)

    src = lax.broadcasted_iota(jnp.int32, (ML_T, ML_T), 0)
    dst = lax.broadcasted_iota(jnp.int32, (ML_T, ML_T), 1)
    lo = dst < HALF
    lo_row = lax.broadcasted_iota(jnp.int32, (1, LANES), 1) < HALF
    causal = src <= dst
    tri = (src >= dst).astype(F32)
    not_first = (step > 0).astype(F32)

    for b in range(BATCH):
        mvt_ref, mot_ref = mvt_refs[b], mot_refs[b]
        xcat_ref[b, 0:ML_HALO, :] = halo_ref[b] * not_first
        xcat_ref[b, ML_HALO:ML_HALO + ML_STEP, :] = qk_ref[b]
        acc = jnp.broadcast_to(cb_ref[...], (ML_STEP, M_QK_W))
        for j in range(M_CONV):
            acc = acc + cw_ref[j:j + 1, :] * xcat_ref[b, pl.ds(ML_HALO - (M_CONV - 1) + j, ML_STEP), :]
        qk_step = acc * jax.nn.sigmoid(acc)

        gb_step = gt_ref[b] + gb_ref[...]
        lf_step = _log_sigmoid(gb_step)
        chunks = [slice(u * ML_T, (u + 1) * ML_T) for u in range(ML_SUB)]
        bcs = [jnp.dot(tri, lf_step[tok], precision=lax.Precision.HIGHEST, preferred_element_type=F32)
               for tok in chunks]
        b_ts = [bc.T for bc in bcs]
        for u, p in [(u, p) for u in range(ML_SUB) for p in range(M_HEADS // 2)]:
            tok, bc, b_t = chunks[u], bcs[u], b_ts[u]
            qk, gb = qk_step[tok], gb_step[tok]
            qp = qk[:, p * LANES:(p + 1) * LANES] * (M_DQK ** -0.5)
            kp = qk[:, M_QK_W // 2 + p * LANES:M_QK_W // 2 + (p + 1) * LANES]
            kp_bf = kp.astype(BF16)
            nrow = n_ref[b * 2 + p][0:1, :]
            n_bf = jnp.broadcast_to(nrow, (SUBLANES, LANES)).astype(BF16)
            n_parts = []
            for hh in range(2):
                h = 2 * p + hh
                sidx = b * M_HEADS + h
                mh = lo if hh == 0 else jnp.logical_not(lo)
                qm_bf = jnp.where(mh, qp, 0.0).astype(BF16)
                s_t = _dot_nt(kp_bf, qm_bf)
                m_prev = m_ref[sidx][0:1, 0:1]
                b_row = b_t[M_HEADS + h:M_HEADS + h + 1, :]
                col = jnp.broadcast_to(gb[:, h:h + 1] - bc[:, M_HEADS + h:M_HEADS + h + 1], (ML_T, ML_T))
                log_d = jnp.where(causal, b_row + col, NEG_INF)
                inter = b_row + m_prev
                m_t = jnp.maximum(inter, jnp.max(log_d, axis=0, keepdims=True))
                sp = s_t * jnp.exp(log_d - m_t)
                w_inter = jnp.exp(inter - m_t)
                qn = _dot_nt(n_bf, qm_bf)[0:1, :]
                den = jnp.sum(sp, axis=0, keepdims=True) + w_inter * qn
                v_t = mvt_ref[h * M_DV:(h + 1) * M_DV, tok]
                c_h = c_ref[sidx]
                num_t = _dot(v_t, sp.astype(BF16)) + w_inter * _dot_nt(c_h.astype(BF16), qm_bf)
                h_t = num_t / jnp.maximum(jnp.abs(den), jnp.exp(-m_t))
                h_n = h_t * lax.rsqrt(jnp.mean(h_t * h_t, axis=0, keepdims=True) + EPS)
                rows_h = slice(h * M_DV, (h + 1) * M_DV)
                out_t = jax.nn.sigmoid(mot_ref[rows_h, tok]) * (h_n * gmh_ref[rows_h, :])
                hm_ref[b, tok, h * M_DV:(h + 1) * M_DV] = out_t.T.astype(BF16)
                m_new = m_t[:, ML_T - 1:ML_T]
                b_last = b_row[:, ML_T - 1:ML_T]
                decay = jnp.exp(b_last + m_prev - m_new)
                wk = jnp.exp(col + (b_last - m_new)) * kp
                c_ref[sidx] = decay * c_h + jnp.where(mh, _dot(v_t, wk.astype(BF16)), 0.0)
                n_parts.append(decay * nrow + jnp.sum(wk, axis=0, keepdims=True))
                m_ref[sidx] = jnp.broadcast_to(m_new, (SUBLANES, LANES))
            n_new = jnp.where(lo_row, n_parts[0], n_parts[1])
            n_ref[b * 2 + p] = jnp.broadcast_to(n_new, (SUBLANES, LANES))


def _mlstm(qk_pre, mv_t, mo_t, gates, conv_w, conv_b, gate_bias, g_mhead_b):
    nblk = SEQ // ML_STEP
    blk = lambda c: pl.BlockSpec((BATCH, ML_STEP, c), lambda i: (0, i, 0))
    full = lambda a: pl.BlockSpec(a.shape, lambda i: (0,) * a.ndim)
    halo = pl.BlockSpec((BATCH, ML_HALO, M_QK_W),
                        lambda i: (0, jnp.maximum(i * (ML_STEP // ML_HALO) - 1, 0), 0))
    t_blk = lambda b: pl.BlockSpec((M_V_W, ML_STEP), lambda i: (0, b * nblk + i))
    return pl.pallas_call(
        _mlstm_kernel,
        grid=(nblk,),
        in_specs=[blk(M_QK_W), halo, t_blk(0), t_blk(1), t_blk(0), t_blk(1), blk(LANES),
                  full(conv_w), full(conv_b), full(gate_bias), full(g_mhead_b)],
        out_specs=blk(M_V_W),
        out_shape=jax.ShapeDtypeStruct((BATCH, SEQ, M_V_W), BF16),
        scratch_shapes=[
            pltpu.VMEM((BATCH, ML_HALO + ML_STEP, M_QK_W), F32),
            pltpu.VMEM((BATCH * M_HEADS, M_DV, LANES), F32),
            pltpu.VMEM((BATCH * 2, SUBLANES, LANES), F32),
            pltpu.VMEM((BATCH * M_HEADS, SUBLANES, LANES), F32),
        ],
        compiler_params=_params(1, 32),
        name="mlstm",
    )(qk_pre, qk_pre, mv_t, mv_t, mo_t, mo_t, gates, conv_w, conv_b, gate_bias, g_mhead_b)


SWA_SUB = 8
SWA_T = SWA_SUB * A_BLOCK


def _swa_kernel(q_ref, kp_ref, kc_ref, vtp_ref, vtc_ref, sk_ref, o_ref):
    n = pl.program_id(1)
    lo = lax.broadcasted_iota(jnp.int32, (A_BLOCK, LANES), 1) < HALF
    top = lax.broadcasted_iota(jnp.int32, (LANES, A_BLOCK), 0) < HALF
    kj = lax.broadcasted_iota(jnp.int32, (2 * A_BLOCK, A_BLOCK), 0)
    qi = lax.broadcasted_iota(jnp.int32, (2 * A_BLOCK, A_BLOCK), 1)
    diff = qi - kj + A_BLOCK
    band = (diff >= 0) & (diff < WINDOW)
    band_first = band & ((kj >= A_BLOCK) | (n > 0))
    for sb in range(SWA_SUB):
        q_rows = slice(sb * A_BLOCK, (sb + 1) * A_BLOCK)
        mask = band_first if sb == 0 else band
        for j in range(A_KV_HEADS):
            cols = slice(j * LANES, (j + 1) * LANES)
            if sb == 0:
                kk = jnp.concatenate([kp_ref[0, :, cols], kc_ref[0, 0:A_BLOCK, cols]], axis=0)
                vt = jnp.concatenate([vtp_ref[cols, :], vtc_ref[cols, 0:A_BLOCK]], axis=1)
            else:
                kv_rows = slice((sb - 1) * A_BLOCK, (sb + 1) * A_BLOCK)
                kk = kc_ref[0, kv_rows, cols]
                vt = vtc_ref[cols, kv_rows]
            for p in range(2):
                c0 = (2 * j + p) * LANES
                qp = q_ref[0, q_rows, c0:c0 + LANES].astype(F32)
                outs = []
                for hh in range(2):
                    h = 4 * j + 2 * p + hh
                    mh = lo if hh == 0 else jnp.logical_not(lo)
                    qm = jnp.where(mh, qp, 0.0).astype(BF16)
                    sc = jnp.where(mask, _dot_nt(kk, qm), NEG_INF)
                    sink = sk_ref[h:h + 1, 0:1]
                    mx = jnp.maximum(jnp.max(sc, axis=0, keepdims=True), sink)
                    pe = jnp.exp(sc - mx)
                    den = jnp.sum(pe, axis=0, keepdims=True) + jnp.exp(sink - mx)
                    outs.append(_dot(vt, pe.astype(BF16)) / den)
                o_t = jnp.where(top, outs[0], outs[1])
                o_ref[0, q_rows, c0:c0 + LANES] = o_t.T.astype(BF16)


def _swa(aq, ak, av_t, sinks_b):
    per_b = SEQ // SWA_T
    cur = lambda c: pl.BlockSpec((1, SWA_T, c), lambda b, n: (b, n, 0))
    prev = lambda c: pl.BlockSpec((1, A_BLOCK, c), lambda b, n: (b, jnp.maximum(n * SWA_SUB - 1, 0), 0))
    vt_cur = pl.BlockSpec((A_KV_W, SWA_T), lambda b, n: (0, b * per_b + n))
    vt_prev = pl.BlockSpec((A_KV_W, A_BLOCK),
                           lambda b, n: (0, b * per_b * SWA_SUB + jnp.maximum(n * SWA_SUB - 1, 0)))
    return pl.pallas_call(
        _swa_kernel,
        grid=(BATCH, per_b),
        in_specs=[cur(A_Q_W), prev(A_KV_W), cur(A_KV_W), vt_prev, vt_cur,
                  pl.BlockSpec(sinks_b.shape, lambda b, n: (0, 0))],
        out_specs=cur(A_Q_W),
        out_shape=jax.ShapeDtypeStruct((BATCH, SEQ, A_Q_W), BF16),
        compiler_params=_params(2, 32),
        name="swa",
    )(aq, ak, ak, av_t, av_t, sinks_b)


def _memkv_kernel(mem_ref, g_ref, wk_ref, wv_ref, k_ref, v_ref):
    mn = _rms(mem_ref[0], g_ref[...]).astype(BF16)
    k_ref[0] = _dot(mn, wk_ref[...]).astype(BF16)
    v_ref[0] = _dot(mn, wv_ref[...]).astype(BF16)


def _memkv(mem, g_mem, w_xk, w_xv):
    blk = pl.BlockSpec((1, MEM_LEN, D_MODEL), lambda b: (b, 0, 0))
    full = lambda a: pl.BlockSpec(a.shape, lambda b: (0,) * a.ndim)
    return pl.pallas_call(
        _memkv_kernel,
        grid=(BATCH,),
        in_specs=[blk, full(g_mem), full(w_xk), full(w_xv)],
        out_specs=[blk, blk],
        out_shape=[jax.ShapeDtypeStruct((BATCH, MEM_LEN, D_MODEL), BF16)] * 2,
        compiler_params=_params(1, 32),
        name="memkv",
    )(mem, g_mem, w_xk, w_xv)


POST_TM = 512


def _post_kernel(x_ref, hm_ref, ha_ref, wo_ref, gc_ref, wq_ref, km_ref, vm_ref, wxo_ref, gf_ref,
                 h2_ref, xn3_ref):
    h1 = x_ref[...] + _dot(hm_ref[...], wo_ref[0:M_V_W, :]) + _dot(ha_ref[...], wo_ref[M_V_W:M_V_W + A_Q_W, :])
    xn2 = _rms(h1, gc_ref[...]).astype(BF16)
    q = _dot(xn2, wq_ref[...]).astype(BF16)
    heads = []
    for hd in range(X_HEADS):
        cols = slice(hd * X_DH, (hd + 1) * X_DH)
        sc = _dot_nt(q[:, cols], km_ref[0, :, cols])
        mx = jnp.max(sc, axis=-1, keepdims=True)
        pe = jnp.exp(sc - mx)
        den = jnp.sum(pe, axis=-1, keepdims=True)
        heads.append((_dot(pe.astype(BF16), vm_ref[0, :, cols]) / den).astype(BF16))
    h2 = h1 + _dot(jnp.concatenate(heads, axis=1), wxo_ref[...])
    h2_ref[...] = h2
    xn3_ref[...] = _rms(h2, gf_ref[...]).astype(BF16)


def _post(x2, hm2, ha2, w_out, g_cross, w_xq, kmem, vmem, w_xo, g_ffn):
    n = x2.shape[0]
    per_batch = SEQ // POST_TM
    row = lambda c: pl.BlockSpec((POST_TM, c), lambda i: (i, 0))
    full = lambda a: pl.BlockSpec(a.shape, lambda i: (0,) * a.ndim)
    memblk = pl.BlockSpec((1, MEM_LEN, D_MODEL), lambda i: (i // per_batch, 0, 0))
    return pl.pallas_call(
        _post_kernel,
        grid=(n // POST_TM,),
        in_specs=[row(D_MODEL), row(M_V_W), row(A_Q_W), full(w_out), full(g_cross), full(w_xq),
                  memblk, memblk, full(w_xo), full(g_ffn)],
        out_specs=[row(D_MODEL), row(D_MODEL)],
        out_shape=[jax.ShapeDtypeStruct((n, D_MODEL), F32), jax.ShapeDtypeStruct((n, D_MODEL), BF16)],
        compiler_params=_params(1, 48),
        name="post",
    )(x2, hm2, ha2, w_out, g_cross, w_xq, kmem, vmem, w_xo, g_ffn)


RT_T = 512
RT_G = LANES


def _ce(vals, i, j):
    a, b = vals[i], vals[j]
    vals[i] = jnp.maximum(a, b)
    vals[j] = jnp.minimum(a, b)


def _bitonic_sort_desc(vals):
    n = len(vals)
    k = 2
    while k <= n:
        j = k // 2
        while j >= 1:
            for i in range(n):
                l = i ^ j
                if l > i:
                    if (i & k) == 0:
                        _ce(vals, i, l)
                    else:
                        _ce(vals, l, i)
            j //= 2
        k *= 2


def _bitonic_merge_desc(vals):
    n = len(vals)
    j = n // 2
    while j >= 1:
        for i in range(n):
            l = i ^ j
            if l > i:
                _ce(vals, i, l)
        j //= 2


def _merge_top16(xs, ys):
    m = [jnp.maximum(xs[a], ys[P_TOPK - 1 - a]) for a in range(P_TOPK)]
    _bitonic_merge_desc(m)
    return m


def _top16_over_keys(vals):
    vals = list(vals)
    _bitonic_sort_desc(vals)
    for sh in (4, 2, 1):
        rolled = [pltpu.roll(v, sh, axis=0) for v in vals]
        vals = _merge_top16(vals, rolled)
    return vals


def _split_bf16(x):
    hi = x.astype(BF16)
    return hi, (x - hi.astype(F32)).astype(BF16)


def _route_kernel(xn_ref, wq_ref, k1_ref, k2_ref, cnt_ref, e1_ref, r2_ref, e2_ref):
    q_t = _dot_nt(wq_ref[...], xn_ref[...])
    k_parts = [_split_bf16(k1_ref[...]), _split_bf16(k2_ref[...])]
    s1_all, s2_all = [], []
    for h in range(P_HEADS):
        for half, dst in enumerate((s1_all, s2_all)):
            r0 = (2 * h + half) * P_KEY_DIM
            q_hi, q_lo = _split_bf16(q_t[r0:r0 + P_KEY_DIM, :])
            k_hi, k_lo = k_parts[half]
            main = _dot(k_hi, jnp.concatenate([q_hi, q_lo], axis=1))
            dst.append(main[:, 0:RT_T] + main[:, RT_T:2 * RT_T] + _dot(k_lo, q_hi))
    for g in range(RT_T // RT_G):
        ls = slice(g * RT_G, (g + 1) * RT_G)
        _route_select([s[:, ls] for s in s1_all], [s[:, ls] for s in s2_all], cnt_ref, e1_ref, r2_ref, e2_ref, ls)


def _route_select(s1_all, s2_all, cnt_ref, e1_ref, r2_ref, e2_ref, ls):
    probs = [s for pair in zip(s1_all, s2_all) for s in pair]
    wide = [jnp.concatenate([s[kb * SUBLANES:(kb + 1) * SUBLANES, :] for s in probs], axis=1)
            for kb in range(P_NKEYS // SUBLANES)]
    tops = _top16_over_keys(wide)
    sub = lax.broadcasted_iota(jnp.int32, (SUBLANES, RT_G), 0)
    v1 = [None] * P_TOPK
    v2 = [None] * P_TOPK
    for a in range(P_TOPK):
        for h in range(P_HEADS):
            t1 = tops[a][:, (2 * h) * RT_G:(2 * h + 1) * RT_G]
            t2 = tops[a][:, (2 * h + 1) * RT_G:(2 * h + 2) * RT_G]
            v1[a] = t1 if h == 0 else jnp.where(sub == h, t1, v1[a])
            v2[a] = t2 if h == 0 else jnp.where(sub == h, t2, v2[a])
    pairs = [(a, b) for a in range(P_TOPK) for b in range(P_TOPK // (a + 1))]
    cand_of = {ab: v1[ab[0]] + v2[ab[1]] for ab in pairs}
    pad = jnp.full((SUBLANES, RT_G), NEG_INF, F32)
    cands = [cand_of[ab] for ab in pairs] + [pad] * (64 - len(pairs))
    groups = []
    for gi in range(4):
        grp = cands[gi::4]
        _bitonic_sort_desc(grp)
        groups.append(grp)
    top = _merge_top16(_merge_top16(groups[0], groups[1]), _merge_top16(groups[2], groups[3]))
    tau = top[P_TOPK - 1]
    z = jnp.ones_like(tau)
    for kk in range(1, P_TOPK):
        z = z + jnp.exp(top[kk] - top[0])
    zinv = 1.0 / z
    cnt_rank = []
    for a in range(P_TOPK):
        c_a = jnp.zeros((SUBLANES, RT_G), F32)
        for b in range(P_TOPK // (a + 1)):
            c_a = jnp.where(cand_of[(a, b)] >= tau, float(b + 1), c_a)
        cnt_rank.append(c_a)
    for h in range(P_HEADS):
        s1, s2 = s1_all[h], s2_all[h]
        cnt_rows = jnp.zeros((P_NKEYS, RT_G), F32)
        for a in range(P_TOPK):
            cnt_rows = jnp.where(s1 == v1[a][h:h + 1, :], cnt_rank[a][h:h + 1, :], cnt_rows)
        rank2 = jnp.zeros((P_NKEYS, RT_G), F32)
        for b in range(P_TOPK):
            rank2 = jnp.where(s2 < v2[b][h:h + 1, :], float(b + 1), rank2)
        cnt_ref[h, :, ls] = cnt_rows
        e1_ref[h, :, ls] = jnp.exp(s1 - v1[0][h:h + 1, :]) * zinv[h:h + 1, :]
        r2_ref[h, :, ls] = pltpu.bitcast(rank2.astype(BF16), jnp.uint32)
        e2_ref[h, :, ls] = pltpu.bitcast(jnp.exp(s2 - v2[0][h:h + 1, :]).astype(BF16), jnp.uint32)


def _route(xn3, w_pq_t, keys1, keys2):
    n = xn3.shape[0]
    full = lambda a: pl.BlockSpec(a.shape, lambda i: (0,) * a.ndim)
    oblk = lambda rows: pl.BlockSpec((P_HEADS, rows, RT_T), lambda i: (0, 0, i))
    f32_rows = jax.ShapeDtypeStruct((P_HEADS, P_NKEYS, n), F32)
    packed_rows = jax.ShapeDtypeStruct((P_HEADS, P_NKEYS // 2, n), jnp.uint32)
    return pl.pallas_call(
        _route_kernel,
        grid=(n // RT_T,),
        in_specs=[pl.BlockSpec((RT_T, D_MODEL), lambda i: (i, 0)), full(w_pq_t), full(keys1), full(keys2)],
        out_specs=[oblk(P_NKEYS), oblk(P_NKEYS), oblk(P_NKEYS // 2), oblk(P_NKEYS // 2)],
        out_shape=[f32_rows, f32_rows, packed_rows, packed_rows],
        compiler_params=_params(1, 40),
        name="route",
    )(xn3, w_pq_t, keys1, keys2)


PE_T = 1024
PE_E = 1024
PE_NE = P_EXPERTS // PE_E
PE_ROWS = PE_E // P_NKEYS
_SQRT_HALF = 0.7071067811865476


def _peer_kernel(dn_ref, up_ref, xn_ref, cnt_ref, e1_ref, r2_ref, e2_ref, h2_ref, gfin_ref, o_ref,
                 a_buf, g_buf, acc_ref):
    s = pl.program_id(0)
    slot = lax.rem(s, 2)
    other = 1 - slot
    up_tile = lax.rem(s - 2, PE_NE)

    @pl.when(s == 0)
    def _():
        a_buf[1] = jnp.zeros((PE_E, PE_T), BF16)
        g_buf[0] = jnp.zeros((PE_E, PE_T), BF16)

    @pl.when((s == 0) | ((s >= 2) & (up_tile == 0)))
    def _():
        acc_ref[...] = jnp.zeros_like(acc_ref)

    a_buf[slot] = _dot_nt(dn_ref[...], xn_ref[...]).astype(BF16)
    acc_ref[...] += _dot(up_ref[...], g_buf[slot])

    n_slab = P_NKEYS // BF16_ROWS
    zero = jnp.zeros((), BF16)
    for c in range(PE_T // LANES):
        cs = slice(c * LANES, (c + 1) * LANES)
        for i in range(PE_ROWS):
            coef = [jnp.zeros((BF16_ROWS, LANES), BF16)] * n_slab
            for h in range(P_HEADS):
                cnt_t = jnp.broadcast_to(cnt_ref[h, 0, i:i + 1, cs], (BF16_ROWS, LANES)).astype(BF16)
                e1_t = jnp.broadcast_to(e1_ref[h, 0, i:i + 1, cs], (BF16_ROWS, LANES)).astype(BF16)
                for k in range(n_slab):
                    js = slice(k * SUBLANES, (k + 1) * SUBLANES)
                    r2 = pltpu.bitcast(r2_ref[h, js, cs], BF16)
                    e2 = pltpu.bitcast(e2_ref[h, js, cs], BF16)
                    coef[k] = coef[k] + jnp.where(r2 < cnt_t, e2, zero) * e1_t
            for k in range(n_slab):
                rows = slice(i * P_NKEYS + k * BF16_ROWS, i * P_NKEYS + (k + 1) * BF16_ROWS)
                a = a_buf[other, rows, cs]
                gl = 0.5 * a * (1.0 + lax.erf(a * _SQRT_HALF))
                g_buf[other, rows, cs] = gl * coef[k]

    @pl.when((s >= 2) & (up_tile == PE_NE - 1))
    def _():
        o_ref[...] = _rms(h2_ref[...] + acc_ref[...].T, gfin_ref[...])


def _peer(dn, up_t, xn3, cnt, e1, r2, e2, h2, g_final):
    n = xn3.shape[0]
    assert PE_ROWS % SUBLANES == 0
    n_pairs = (n // PE_T) * PE_NE
    pair = lambda s, lag: jnp.clip(s - lag, 0, n_pairs - 1)
    tok = lambda s, lag: pair(s, lag) // PE_NE
    tile = lambda s, lag: pair(s, lag) % PE_NE
    cnt, e1 = (a.reshape(P_HEADS, PE_NE, PE_ROWS, n) for a in (cnt, e1))
    rblk = pl.BlockSpec((P_HEADS, 1, PE_ROWS, PE_T), lambda s: (0, tile(s, 1), 0, tok(s, 1)))
    pblk = pl.BlockSpec((P_HEADS, P_NKEYS // 2, PE_T), lambda s: (0, 0, tok(s, 1)))
    return pl.pallas_call(
        _peer_kernel,
        grid=(n_pairs + 2,),
        in_specs=[pl.BlockSpec((PE_E, D_MODEL), lambda s: (tile(s, 0), 0)),
                  pl.BlockSpec((D_MODEL, PE_E), lambda s: (0, tile(s, 2))),
                  pl.BlockSpec((PE_T, D_MODEL), lambda s: (tok(s, 0), 0)),
                  rblk, rblk, pblk, pblk,
                  pl.BlockSpec((PE_T, D_MODEL), lambda s: (tok(s, 2), 0)),
                  pl.BlockSpec(g_final.shape, lambda s: (0, 0))],
        out_specs=pl.BlockSpec((PE_T, D_MODEL), lambda s: (tok(s, 2), 0)),
        out_shape=jax.ShapeDtypeStruct((n, D_MODEL), F32),
        scratch_shapes=[pltpu.VMEM((2, PE_E, PE_T), BF16), pltpu.VMEM((2, PE_E, PE_T), BF16),
                        pltpu.VMEM((D_MODEL, PE_T), F32)],
        compiler_params=_params(1, 56),
        name="peer",
    )(dn, up_t, xn3, cnt, e1, r2, e2, h2, g_final)


def _pack_w_in(w):
    o = 0
    segs = {}
    for name, sz in (("mq", M_QK_W // 2), ("mk", M_QK_W // 2), ("mv", M_V_W), ("mo", M_V_W),
                     ("mi", M_HEADS), ("mf", M_HEADS), ("aq", A_Q_W), ("ak", A_KV_HEADS * A_DH),
                     ("av", A_KV_HEADS * A_DH)):
        segs[name] = w[:, o:o + sz]
        o += sz
    k0, k1 = segs["ak"][:, :A_DH], segs["ak"][:, A_DH:]
    v0, v1 = segs["av"][:, :A_DH], segs["av"][:, A_DH:]
    gates = jnp.concatenate([segs["mi"], segs["mf"], jnp.zeros((w.shape[0], LANES - 2 * M_HEADS), w.dtype)], axis=1)
    packed = jnp.concatenate([segs["mq"], segs["mk"], segs["aq"] * (A_DH ** -0.5), k0, k0, k1, k1, gates], axis=1)
    w_t = jnp.concatenate([segs["mv"], segs["mo"], v0, v0, v1, v1], axis=1).T
    return packed.astype(BF16), w_t.astype(BF16)


def kernel(x, mem, g_mix, w_in, conv_w, conv_b, b_igate, b_fgate, g_mhead, sinks, w_out, g_cross, g_mem,
           w_xq, w_xk, w_xv, w_xo, g_ffn, w_pq, sub_keys1, sub_keys2, expert_down, expert_up, g_final):
    assert x.shape == (BATCH, SEQ, D_MODEL) and w_in.shape[0] == 1
    l = 0
    row = lambda v: v.reshape(1, -1).astype(F32)
    x2 = x.reshape(NTOK, D_MODEL)

    qk_pre, aq, ak, gates, mv_t, mo_t, av_t = _mix_in(x2, row(g_mix[l]), *_pack_w_in(w_in[l]))

    gate_bias = jnp.concatenate([b_igate[l], b_fgate[l], jnp.zeros((LANES - 2 * M_HEADS,), F32)]).reshape(1, LANES)
    g_mhead_b = jnp.broadcast_to(g_mhead[l].astype(F32)[:, None], (M_V_W, LANES))
    r3 = lambda a: a.reshape(BATCH, SEQ, a.shape[-1])
    hm = _mlstm(r3(qk_pre), mv_t, mo_t, r3(gates), conv_w[l], row(conv_b[l]), gate_bias, g_mhead_b)

    sinks_b = jnp.broadcast_to(sinks[l].astype(F32)[:, None], (A_HEADS, LANES))
    ha = _swa(r3(aq), r3(ak), av_t, sinks_b)

    kmem, vmem = _memkv(mem, row(g_mem[l]), w_xk[l].astype(BF16), w_xv[l].astype(BF16))

    h2, xn3 = _post(x2, hm.reshape(NTOK, M_V_W), ha.reshape(NTOK, A_Q_W), w_out[l].astype(BF16), row(g_cross[l]),
                    (w_xq[l] * (X_DH ** -0.5)).astype(BF16), kmem, vmem, w_xo[l].astype(BF16), row(g_ffn[l]))

    cnt, e1, r2, e2 = _route(xn3, w_pq[l].T.astype(BF16), sub_keys1[l], sub_keys2[l])

    out = _peer(expert_down[l].astype(BF16), expert_up[l].T.astype(BF16), xn3, cnt, e1, r2, e2, h2, row(g_final))
    return out.reshape(BATCH, SEQ, D_MODEL)
```

```python
import jax
import jax.numpy as jnp
from jax import lax
from jax.experimental import pallas as pl
from jax.experimental.pallas import tpu as pltpu

F32 = jnp.float32
BF16 = jnp.bfloat16

D_MODEL = 1024
BATCH = 2
SEQ = 8192
NTOK = BATCH * SEQ
MEM_LEN = 256
EPS = 1e-6

M_HEADS = 4
M_DV = 128
M_DQK = 64
M_CONV = 4
M_QK_W = 2 * M_HEADS * M_DQK
M_V_W = M_HEADS * M_DV

A_HEADS = 8
A_KV_HEADS = 2
A_DH = 64
WINDOW = 128
A_BLOCK = 128
A_Q_W = A_HEADS * A_DH
A_KV_W = 2 * A_KV_HEADS * A_DH

X_HEADS = 4
X_DH = 256

P_HEADS = 8
P_NKEYS = 128
P_EXPERTS = P_NKEYS * P_NKEYS
P_KEY_DIM = 128
P_TOPK = 16

LANES = 128
HALF = LANES // 2
SUBLANES = 8
BF16_ROWS = 2 * SUBLANES

NEG_INF = float("-inf")

V7X_VMEM_BYTES = 64 * 1024 * 1024
MIB = 1024 * 1024


def _params(ndims, vmem_mib):
    assert vmem_mib * MIB < V7X_VMEM_BYTES
    return pltpu.CompilerParams(dimension_semantics=("arbitrary",) * ndims, vmem_limit_bytes=vmem_mib * MIB)


def _rms(xf, g):
    return xf * lax.rsqrt(jnp.mean(xf * xf, axis=-1, keepdims=True) + EPS) * g


def _dot_nt(a, b):
    return lax.dot_general(a, b, (((1,), (1,)), ((), ())), preferred_element_type=F32)


def _dot(a, b):
    return jnp.dot(a, b, preferred_element_type=F32)


MIX_TM = 512
MIX_COLS = (0, M_QK_W, M_QK_W + A_Q_W, M_QK_W + A_Q_W + A_KV_W, M_QK_W + A_Q_W + A_KV_W + LANES)
MIX_ROWS = (0, M_V_W, 2 * M_V_W, 2 * M_V_W + A_KV_W)


def _mix_in_kernel(x_ref, g_ref, w_ref, wt_ref, qk_ref, aq_ref, ak_ref, gt_ref, mvt_ref, mot_ref, avt_ref):
    xn = _rms(x_ref[...], g_ref[...]).astype(BF16)
    c, r = MIX_COLS, MIX_ROWS
    qk_ref[...] = _dot(xn, w_ref[:, c[0]:c[1]])
    aq_ref[...] = _dot(xn, w_ref[:, c[1]:c[2]]).astype(BF16)
    ak_ref[...] = _dot(xn, w_ref[:, c[2]:c[3]]).astype(BF16)
    gt_ref[...] = _dot(xn, w_ref[:, c[3]:c[4]])
    mvt_ref[...] = _dot_nt(wt_ref[r[0]:r[1], :], xn).astype(BF16)
    mot_ref[...] = _dot_nt(wt_ref[r[1]:r[2], :], xn)
    avt_ref[...] = _dot_nt(wt_ref[r[2]:r[3], :], xn).astype(BF16)


def _mix_in(x2, g_mix, w_all, w_t):
    n = x2.shape[0]
    row = lambda c: pl.BlockSpec((MIX_TM, c), lambda i: (i, 0))
    col = lambda r: pl.BlockSpec((r, MIX_TM), lambda i: (0, i))
    full = lambda a: pl.BlockSpec(a.shape, lambda i: (0,) * a.ndim)
    return pl.pallas_call(
        _mix_in_kernel,
        grid=(n // MIX_TM,),
        in_specs=[row(D_MODEL), full(g_mix), full(w_all), full(w_t)],
        out_specs=[row(M_QK_W), row(A_Q_W), row(A_KV_W), row(LANES), col(M_V_W), col(M_V_W), col(A_KV_W)],
        out_shape=[
            jax.ShapeDtypeStruct((n, M_QK_W), F32),
            jax.ShapeDtypeStruct((n, A_Q_W), BF16),
            jax.ShapeDtypeStruct((n, A_KV_W), BF16),
            jax.ShapeDtypeStruct((n, LANES), F32),
            jax.ShapeDtypeStruct((M_V_W, n), BF16),
            jax.ShapeDtypeStruct((M_V_W, n), F32),
            jax.ShapeDtypeStruct((A_KV_W, n), BF16),
        ],
        compiler_params=_params(1, 40),
        name="mix_in",
    )(x2, g_mix, w_all, w_t)


ML_T = 128
ML_SUB = 4
ML_STEP = ML_SUB * ML_T
ML_HALO = 8


def _log_sigmoid(x):
    return jnp.minimum(x, 0.0) - jnp.log1p(jnp.exp(-jnp.abs(x)))


def _mlstm_kernel(qk_ref, halo_ref, mvt0_ref, mvt1_ref, mot0_ref, mot1_ref, gt_ref, cw_ref, cb_ref, gb_ref,
                  gmh_ref, hm_ref, xcat_ref, c_ref, n_ref, m_ref):
    step = pl.program_id(0)
    mvt_refs = (mvt0_ref, mvt1_ref)
    mot_refs = (mot0_ref, mot1_ref)

    @pl.when(step == 0)
    def _():
        c_ref[...] = jnp.zeros_like(c_ref)
        n_ref[...] = jnp.zeros_like(n_ref)
        m_ref[...] = jnp.zeros_like(m_ref)

    src = lax.broadcasted_iota(jnp.int32, (ML_T, ML_T), 0)
    dst = lax.broadcasted_iota(jnp.int32, (ML_T, ML_T), 1)
    lo = dst < HALF
    lo_row = lax.broadcasted_iota(jnp.int32, (1, LANES), 1) < HALF
    causal = src <= dst
    tri = (src >= dst).astype(F32)
    not_first = (step > 0).astype(F32)

    for b in range(BATCH):
        mvt_ref, mot_ref = mvt_refs[b], mot_refs[b]
        xcat_ref[b, 0:ML_HALO, :] = halo_ref[b] * not_first
        xcat_ref[b, ML_HALO:ML_HALO + ML_STEP, :] = qk_ref[b]
        acc = jnp.broadcast_to(cb_ref[...], (ML_STEP, M_QK_W))
        for j in range(M_CONV):
            acc = acc + cw_ref[j:j + 1, :] * xcat_ref[b, pl.ds(ML_HALO - (M_CONV - 1) + j, ML_STEP), :]
        qk_step = acc * jax.nn.sigmoid(acc)

        gb_step = gt_ref[b] + gb_ref[...]
        lf_step = _log_sigmoid(gb_step)
        chunks = [slice(u * ML_T, (u + 1) * ML_T) for u in range(ML_SUB)]
        bcs = [jnp.dot(tri, lf_step[tok], precision=lax.Precision.HIGHEST, preferred_element_type=F32)
               for tok in chunks]
        b_ts = [bc.T for bc in bcs]
        for u, p in [(u, p) for u in range(ML_SUB) for p in range(M_HEADS // 2)]:
            tok, bc, b_t = chunks[u], bcs[u], b_ts[u]
            qk, gb = qk_step[tok], gb_step[tok]
            qp = qk[:, p * LANES:(p + 1) * LANES] * (M_DQK ** -0.5)
            kp = qk[:, M_QK_W // 2 + p * LANES:M_QK_W // 2 + (p + 1) * LANES]
            kp_bf = kp.astype(BF16)
            nrow = n_ref[b * 2 + p][0:1, :]
            n_bf = jnp.broadcast_to(nrow, (SUBLANES, LANES)).astype(BF16)
            n_parts = []
            for hh in range(2):
                h = 2 * p + hh
                sidx = b * M_HEADS + h
                mh = lo if hh == 0 else jnp.logical_not(lo)
                qm_bf = jnp.where(mh, qp, 0.0).astype(BF16)
                s_t = _dot_nt(kp_bf, qm_bf)
                m_prev = m_ref[sidx][0:1, 0:1]
                b_row = b_t[M_HEADS + h:M_HEADS + h + 1, :]
                col = jnp.broadcast_to(gb[:, h:h + 1] - bc[:, M_HEADS + h:M_HEADS + h + 1], (ML_T, ML_T))
                log_d = jnp.where(causal, b_row + col, NEG_INF)
                inter = b_row + m_prev
                m_t = jnp.maximum(inter, jnp.max(log_d, axis=0, keepdims=True))
                sp = s_t * jnp.exp(log_d - m_t)
                w_inter = jnp.exp(inter - m_t)
                qn = _dot_nt(n_bf, qm_bf)[0:1, :]
                den = jnp.sum(sp, axis=0, keepdims=True) + w_inter * qn
                v_t = mvt_ref[h * M_DV:(h + 1) * M_DV, tok]
                c_h = c_ref[sidx]
                num_t = _dot(v_t, sp.astype(BF16)) + w_inter * _dot_nt(c_h.astype(BF16), qm_bf)
                h_t = num_t / jnp.maximum(jnp.abs(den), jnp.exp(-m_t))
                h_n = h_t * lax.rsqrt(jnp.mean(h_t * h_t, axis=0, keepdims=True) + EPS)
                rows_h = slice(h * M_DV, (h + 1) * M_DV)
                out_t = jax.nn.sigmoid(mot_ref[rows_h, tok]) * (h_n * gmh_ref[rows_h, :])
                hm_ref[b, tok, h * M_DV:(h + 1) * M_DV] = out_t.T.astype(BF16)
                m_new = m_t[:, ML_T - 1:ML_T]
                b_last = b_row[:, ML_T - 1:ML_T]
                decay = jnp.exp(b_last + m_prev - m_new)
                wk = jnp.exp(col + (b_last - m_new)) * kp
                c_ref[sidx] = decay * c_h + jnp.where(mh, _dot(v_t, wk.astype(BF16)), 0.0)
                n_parts.append(decay * nrow + jnp.sum(wk, axis=0, keepdims=True))
                m_ref[sidx] = jnp.broadcast_to(m_new, (SUBLANES, LANES))
            n_new = jnp.where(lo_row, n_parts[0], n_parts[1])
            n_ref[b * 2 + p] = jnp.broadcast_to(n_new, (SUBLANES, LANES))


def _mlstm(qk_pre, mv_t, mo_t, gates, conv_w, conv_b, gate_bias, g_mhead_b):
    nblk = SEQ // ML_STEP
    blk = lambda c: pl.BlockSpec((BATCH, ML_STEP, c), lambda i: (0, i, 0))
    full = lambda a: pl.BlockSpec(a.shape, lambda i: (0,) * a.ndim)
    halo = pl.BlockSpec((BATCH, ML_HALO, M_QK_W),
                        lambda i: (0, jnp.maximum(i * (ML_STEP // ML_HALO) - 1, 0), 0))
    t_blk = lambda b: pl.BlockSpec((M_V_W, ML_STEP), lambda i: (0, b * nblk + i))
    return pl.pallas_call(
        _mlstm_kernel,
        grid=(nblk,),
        in_specs=[blk(M_QK_W), halo, t_blk(0), t_blk(1), t_blk(0), t_blk(1), blk(LANES),
                  full(conv_w), full(conv_b), full(gate_bias), full(g_mhead_b)],
        out_specs=blk(M_V_W),
        out_shape=jax.ShapeDtypeStruct((BATCH, SEQ, M_V_W), BF16),
        scratch_shapes=[
            pltpu.VMEM((BATCH, ML_HALO + ML_STEP, M_QK_W), F32),
            pltpu.VMEM((BATCH * M_HEADS, M_DV, LANES), F32),
            pltpu.VMEM((BATCH * 2, SUBLANES, LANES), F32),
            pltpu.VMEM((BATCH * M_HEADS, SUBLANES, LANES), F32),
        ],
        compiler_params=_params(1, 32),
        name="mlstm",
    )(qk_pre, qk_pre, mv_t, mv_t, mo_t, mo_t, gates, conv_w, conv_b, gate_bias, g_mhead_b)


SWA_SUB = 8
SWA_T = SWA_SUB * A_BLOCK


def _swa_kernel(q_ref, kp_ref, kc_ref, vtp_ref, vtc_ref, sk_ref, o_ref):
    n = pl.program_id(1)
    lo = lax.broadcasted_iota(jnp.int32, (A_BLOCK, LANES), 1) < HALF
    top = lax.broadcasted_iota(jnp.int32, (LANES, A_BLOCK), 0) < HALF
    kj = lax.broadcasted_iota(jnp.int32, (2 * A_BLOCK, A_BLOCK), 0)
    qi = lax.broadcasted_iota(jnp.int32, (2 * A_BLOCK, A_BLOCK), 1)
    diff = qi - kj + A_BLOCK
    band = (diff >= 0) & (diff < WINDOW)
    band_first = band & ((kj >= A_BLOCK) | (n > 0))
    for sb in range(SWA_SUB):
        q_rows = slice(sb * A_BLOCK, (sb + 1) * A_BLOCK)
        mask = band_first if sb == 0 else band
        for j in range(A_KV_HEADS):
            cols = slice(j * LANES, (j + 1) * LANES)
            if sb == 0:
                kk = jnp.concatenate([kp_ref[0, :, cols], kc_ref[0, 0:A_BLOCK, cols]], axis=0)
                vt = jnp.concatenate([vtp_ref[cols, :], vtc_ref[cols, 0:A_BLOCK]], axis=1)
            else:
                kv_rows = slice((sb - 1) * A_BLOCK, (sb + 1) * A_BLOCK)
                kk = kc_ref[0, kv_rows, cols]
                vt = vtc_ref[cols, kv_rows]
            for p in range(2):
                c0 = (2 * j + p) * LANES
                qp = q_ref[0, q_rows, c0:c0 + LANES].astype(F32)
                outs = []
                for hh in range(2):
                    h = 4 * j + 2 * p + hh
                    mh = lo if hh == 0 else jnp.logical_not(lo)
                    qm = jnp.where(mh, qp, 0.0).astype(BF16)
                    sc = jnp.where(mask, _dot_nt(kk, qm), NEG_INF)
                    sink = sk_ref[h:h + 1, 0:1]
                    mx = jnp.maximum(jnp.max(sc, axis=0, keepdims=True), sink)
                    pe = jnp.exp(sc - mx)
                    den = jnp.sum(pe, axis=0, keepdims=True) + jnp.exp(sink - mx)
                    outs.append(_dot(vt, pe.astype(BF16)) / den)
                o_t = jnp.where(top, outs[0], outs[1])
                o_ref[0, q_rows, c0:c0 + LANES] = o_t.T.astype(BF16)


def _swa(aq, ak, av_t, sinks_b):
    per_b = SEQ // SWA_T
    cur = lambda c: pl.BlockSpec((1, SWA_T, c), lambda b, n: (b, n, 0))
    prev = lambda c: pl.BlockSpec((1, A_BLOCK, c), lambda b, n: (b, jnp.maximum(n * SWA_SUB - 1, 0), 0))
    vt_cur = pl.BlockSpec((A_KV_W, SWA_T), lambda b, n: (0, b * per_b + n))
    vt_prev = pl.BlockSpec((A_KV_W, A_BLOCK),
                           lambda b, n: (0, b * per_b * SWA_SUB + jnp.maximum(n * SWA_SUB - 1, 0)))
    return pl.pallas_call(
        _swa_kernel,
        grid=(BATCH, per_b),
        in_specs=[cur(A_Q_W), prev(A_KV_W), cur(A_KV_W), vt_prev, vt_cur,
                  pl.BlockSpec(sinks_b.shape, lambda b, n: (0, 0))],
        out_specs=cur(A_Q_W),
        out_shape=jax.ShapeDtypeStruct((BATCH, SEQ, A_Q_W), BF16),
        compiler_params=_params(2, 32),
        name="swa",
    )(aq, ak, ak, av_t, av_t, sinks_b)


def _memkv_kernel(mem_ref, g_ref, wk_ref, wv_ref, k_ref, v_ref):
    mn = _rms(mem_ref[0], g_ref[...]).astype(BF16)
    k_ref[0] = _dot(mn, wk_ref[...]).astype(BF16)
    v_ref[0] = _dot(mn, wv_ref[...]).astype(BF16)


def _memkv(mem, g_mem, w_xk, w_xv):
    blk = pl.BlockSpec((1, MEM_LEN, D_MODEL), lambda b: (b, 0, 0))
    full = lambda a: pl.BlockSpec(a.shape, lambda b: (0,) * a.ndim)
    return pl.pallas_call(
        _memkv_kernel,
        grid=(BATCH,),
        in_specs=[blk, full(g_mem), full(w_xk), full(w_xv)],
        out_specs=[blk, blk],
        out_shape=[jax.ShapeDtypeStruct((BATCH, MEM_LEN, D_MODEL), BF16)] * 2,
        compiler_params=_params(1, 32),
        name="memkv",
    )(mem, g_mem, w_xk, w_xv)


POST_TM = 512


def _post_kernel(x_ref, hm_ref, ha_ref, wo_ref, gc_ref, wq_ref, km_ref, vm_ref, wxo_ref, gf_ref,
                 h2_ref, xn3_ref):
    h1 = x_ref[...] + _dot(hm_ref[...], wo_ref[0:M_V_W, :]) + _dot(ha_ref[...], wo_ref[M_V_W:M_V_W + A_Q_W, :])
    xn2 = _rms(h1, gc_ref[...]).astype(BF16)
    q = _dot(xn2, wq_ref[...]).astype(BF16)
    heads = []
    for hd in range(X_HEADS):
        cols = slice(hd * X_DH, (hd + 1) * X_DH)
        sc = _dot_nt(q[:, cols], km_ref[0, :, cols])
        mx = jnp.max(sc, axis=-1, keepdims=True)
        pe = jnp.exp(sc - mx)
        den = jnp.sum(pe, axis=-1, keepdims=True)
        heads.append((_dot(pe.astype(BF16), vm_ref[0, :, cols]) / den).astype(BF16))
    h2 = h1 + _dot(jnp.concatenate(heads, axis=1), wxo_ref[...])
    h2_ref[...] = h2
    xn3_ref[...] = _rms(h2, gf_ref[...]).astype(BF16)


def _post(x2, hm2, ha2, w_out, g_cross, w_xq, kmem, vmem, w_xo, g_ffn):
    n = x2.shape[0]
    per_batch = SEQ // POST_TM
    row = lambda c: pl.BlockSpec((POST_TM, c), lambda i: (i, 0))
    full = lambda a: pl.BlockSpec(a.shape, lambda i: (0,) * a.ndim)
    memblk = pl.BlockSpec((1, MEM_LEN, D_MODEL), lambda i: (i // per_batch, 0, 0))
    return pl.pallas_call(
        _post_kernel,
        grid=(n // POST_TM,),
        in_specs=[row(D_MODEL), row(M_V_W), row(A_Q_W), full(w_out), full(g_cross), full(w_xq),
                  memblk, memblk, full(w_xo), full(g_ffn)],
        out_specs=[row(D_MODEL), row(D_MODEL)],
        out_shape=[jax.ShapeDtypeStruct((n, D_MODEL), F32), jax.ShapeDtypeStruct((n, D_MODEL), BF16)],
        compiler_params=_params(1, 48),
        name="post",
    )(x2, hm2, ha2, w_out, g_cross, w_xq, kmem, vmem, w_xo, g_ffn)


RT_T = 512
RT_G = LANES


def _ce(vals, i, j):
    a, b = vals[i], vals[j]
    vals[i] = jnp.maximum(a, b)
    vals[j] = jnp.minimum(a, b)


def _bitonic_sort_desc(vals):
    n = len(vals)
    k = 2
    while k <= n:
        j = k // 2
        while j >= 1:
            for i in range(n):
                l = i ^ j
                if l > i:
                    if (i & k) == 0:
                        _ce(vals, i, l)
                    else:
                        _ce(vals, l, i)
            j //= 2
        k *= 2


def _bitonic_merge_desc(vals):
    n = len(vals)
    j = n // 2
    while j >= 1:
        for i in range(n):
            l = i ^ j
            if l > i:
                _ce(vals, i, l)
        j //= 2


def _merge_top16(xs, ys):
    m = [jnp.maximum(xs[a], ys[P_TOPK - 1 - a]) for a in range(P_TOPK)]
    _bitonic_merge_desc(m)
    return m


def _top16_over_keys(vals):
    vals = list(vals)
    _bitonic_sort_desc(vals)
    for sh in (4, 2, 1):
        rolled = [pltpu.roll(v, sh, axis=0) for v in vals]
        vals = _merge_top16(vals, rolled)
    return vals


def _split_bf16(x):
    hi = x.astype(BF16)
    return hi, (x - hi.astype(F32)).astype(BF16)


def _route_kernel(xn_ref, wq_ref, k1_ref, k2_ref, cnt_ref, e1_ref, r2_ref, e2_ref):
    q_t = _dot_nt(wq_ref[...], xn_ref[...])
    k_parts = [_split_bf16(k1_ref[...]), _split_bf16(k2_ref[...])]
    s1_all, s2_all = [], []
    for h in range(P_HEADS):
        for half, dst in enumerate((s1_all, s2_all)):
            r0 = (2 * h + half) * P_KEY_DIM
            q_hi, q_lo = _split_bf16(q_t[r0:r0 + P_KEY_DIM, :])
            k_hi, k_lo = k_parts[half]
            main = _dot(k_hi, jnp.concatenate([q_hi, q_lo], axis=1))
            dst.append(main[:, 0:RT_T] + main[:, RT_T:2 * RT_T] + _dot(k_lo, q_hi))
    for g in range(RT_T // RT_G):
        ls = slice(g * RT_G, (g + 1) * RT_G)
        _route_select([s[:, ls] for s in s1_all], [s[:, ls] for s in s2_all], cnt_ref, e1_ref, r2_ref, e2_ref, ls)


def _route_select(s1_all, s2_all, cnt_ref, e1_ref, r2_ref, e2_ref, ls):
    probs = [s for pair in zip(s1_all, s2_all) for s in pair]
    wide = [jnp.concatenate([s[kb * SUBLANES:(kb + 1) * SUBLANES, :] for s in probs], axis=1)
            for kb in range(P_NKEYS // SUBLANES)]
    tops = _top16_over_keys(wide)
    sub = lax.broadcasted_iota(jnp.int32, (SUBLANES, RT_G), 0)
    v1 = [None] * P_TOPK
    v2 = [None] * P_TOPK
    for a in range(P_TOPK):
        for h in range(P_HEADS):
            t1 = tops[a][:, (2 * h) * RT_G:(2 * h + 1) * RT_G]
            t2 = tops[a][:, (2 * h + 1) * RT_G:(2 * h + 2) * RT_G]
            v1[a] = t1 if h == 0 else jnp.where(sub == h, t1, v1[a])
            v2[a] = t2 if h == 0 else jnp.where(sub == h, t2, v2[a])
    pairs = [(a, b) for a in range(P_TOPK) for b in range(P_TOPK // (a + 1))]
    cand_of = {ab: v1[ab[0]] + v2[ab[1]] for ab in pairs}
    pad = jnp.full((SUBLANES, RT_G), NEG_INF, F32)
    cands = [cand_of[ab] for ab in pairs] + [pad] * (64 - len(pairs))
    groups = []
    for gi in range(4):
        grp = cands[gi::4]
        _bitonic_sort_desc(grp)
        groups.append(grp)
    top = _merge_top16(_merge_top16(groups[0], groups[1]), _merge_top16(groups[2], groups[3]))
    tau = top[P_TOPK - 1]
    z = jnp.ones_like(tau)
    for kk in range(1, P_TOPK):
        z = z + jnp.exp(top[kk] - top[0])
    zinv = 1.0 / z
    cnt_rank = []
    for a in range(P_TOPK):
        c_a = jnp.zeros((SUBLANES, RT_G), F32)
        for b in range(P_TOPK // (a + 1)):
            c_a = jnp.where(cand_of[(a, b)] >= tau, float(b + 1), c_a)
        cnt_rank.append(c_a)
    for h in range(P_HEADS):
        s1, s2 = s1_all[h], s2_all[h]
        cnt_rows = jnp.zeros((P_NKEYS, RT_G), F32)
        for a in range(P_TOPK):
            cnt_rows = jnp.where(s1 == v1[a][h:h + 1, :], cnt_rank[a][h:h + 1, :], cnt_rows)
        rank2 = jnp.zeros((P_NKEYS, RT_G), F32)
        for b in range(P_TOPK):
            rank2 = jnp.where(s2 < v2[b][h:h + 1, :], float(b + 1), rank2)
        cnt_ref[h, :, ls] = cnt_rows
        e1_ref[h, :, ls] = jnp.exp(s1 - v1[0][h:h + 1, :]) * zinv[h:h + 1, :]
        r2_ref[h, :, ls] = pltpu.bitcast(rank2.astype(BF16), jnp.uint32)
        e2_ref[h, :, ls] = pltpu.bitcast(jnp.exp(s2 - v2[0][h:h + 1, :]).astype(BF16), jnp.uint32)


def _route(xn3, w_pq_t, keys1, keys2):
    n = xn3.shape[0]
    full = lambda a: pl.BlockSpec(a.shape, lambda i: (0,) * a.ndim)
    oblk = lambda rows: pl.BlockSpec((P_HEADS, rows, RT_T), lambda i: (0, 0, i))
    f32_rows = jax.ShapeDtypeStruct((P_HEADS, P_NKEYS, n), F32)
    packed_rows = jax.ShapeDtypeStruct((P_HEADS, P_NKEYS // 2, n), jnp.uint32)
    return pl.pallas_call(
        _route_kernel,
        grid=(n // RT_T,),
        in_specs=[pl.BlockSpec((RT_T, D_MODEL), lambda i: (i, 0)), full(w_pq_t), full(keys1), full(keys2)],
        out_specs=[oblk(P_NKEYS), oblk(P_NKEYS), oblk(P_NKEYS // 2), oblk(P_NKEYS // 2)],
        out_shape=[f32_rows, f32_rows, packed_rows, packed_rows],
        compiler_params=_params(1, 40),
        name="route",
    )(xn3, w_pq_t, keys1, keys2)


PE_T = 1024
PE_E = 1024
PE_NE = P_EXPERTS // PE_E
PE_ROWS = PE_E // P_NKEYS
_SQRT_HALF = 0.7071067811865476


def _peer_kernel(dn_ref, up_ref, xn_ref, cnt_ref, e1_ref, r2_ref, e2_ref, h2_ref, gfin_ref, o_ref,
                 a_buf, g_buf, acc_ref):
    s = pl.program_id(0)
    slot = lax.rem(s, 2)
    other = 1 - slot
    up_tile = lax.rem(s - 2, PE_NE)

    @pl.when(s == 0)
    def _():
        a_buf[1] = jnp.zeros((PE_E, PE_T), BF16)
        g_buf[0] = jnp.zeros((PE_E, PE_T), BF16)

    @pl.when((s == 0) | ((s >= 2) & (up_tile == 0)))
    def _():
        acc_ref[...] = jnp.zeros_like(acc_ref)

    a_buf[slot] = _dot_nt(dn_ref[...].astype(BF16), xn_ref[...]).astype(BF16)
    acc_ref[...] += _dot(up_ref[...], g_buf[slot])

    n_slab = P_NKEYS // BF16_ROWS
    zero = jnp.zeros((), BF16)
    for c in range(PE_T // LANES):
        cs = slice(c * LANES, (c + 1) * LANES)
        for i in range(PE_ROWS):
            coef = [jnp.zeros((BF16_ROWS, LANES), BF16)] * n_slab
            for h in range(P_HEADS):
                cnt_t = jnp.broadcast_to(cnt_ref[h, 0, i:i + 1, cs], (BF16_ROWS, LANES)).astype(BF16)
                e1_t = jnp.broadcast_to(e1_ref[h, 0, i:i + 1, cs], (BF16_ROWS, LANES)).astype(BF16)
                for k in range(n_slab):
                    js = slice(k * SUBLANES, (k + 1) * SUBLANES)
                    r2 = pltpu.bitcast(r2_ref[h, js, cs], BF16)
                    e2 = pltpu.bitcast(e2_ref[h, js, cs], BF16)
                    coef[k] = coef[k] + jnp.where(r2 < cnt_t, e2, zero) * e1_t
            for k in range(n_slab):
                rows = slice(i * P_NKEYS + k * BF16_ROWS, i * P_NKEYS + (k + 1) * BF16_ROWS)
                a = a_buf[other, rows, cs]
                gl = 0.5 * a * (1.0 + lax.erf(a * _SQRT_HALF))
                g_buf[other, rows, cs] = gl * coef[k]

    @pl.when((s >= 2) & (up_tile == PE_NE - 1))
    def _():
        o_ref[...] = _rms(h2_ref[...] + acc_ref[...].T, gfin_ref[...])


def _peer(dn, up_t, xn3, cnt, e1, r2, e2, h2, g_final):
    n = xn3.shape[0]
    assert PE_ROWS % SUBLANES == 0
    n_pairs = (n // PE_T) * PE_NE
    pair = lambda s, lag: jnp.clip(s - lag, 0, n_pairs - 1)
    tok = lambda s, lag: pair(s, lag) // PE_NE
    tile = lambda s, lag: pair(s, lag) % PE_NE
    cnt, e1 = (a.reshape(P_HEADS, PE_NE, PE_ROWS, n) for a in (cnt, e1))
    rblk = pl.BlockSpec((P_HEADS, 1, PE_ROWS, PE_T), lambda s: (0, tile(s, 1), 0, tok(s, 1)))
    pblk = pl.BlockSpec((P_HEADS, P_NKEYS // 2, PE_T), lambda s: (0, 0, tok(s, 1)))
    return pl.pallas_call(
        _peer_kernel,
        grid=(n_pairs + 2,),
        in_specs=[pl.BlockSpec((PE_E, D_MODEL), lambda s: (tile(s, 0), 0)),
                  pl.BlockSpec((D_MODEL, PE_E), lambda s: (0, tile(s, 2))),
                  pl.BlockSpec((PE_T, D_MODEL), lambda s: (tok(s, 0), 0)),
                  rblk, rblk, pblk, pblk,
                  pl.BlockSpec((PE_T, D_MODEL), lambda s: (tok(s, 2), 0)),
                  pl.BlockSpec(g_final.shape, lambda s: (0, 0))],
        out_specs=pl.BlockSpec((PE_T, D_MODEL), lambda s: (tok(s, 2), 0)),
        out_shape=jax.ShapeDtypeStruct((n, D_MODEL), F32),
        scratch_shapes=[pltpu.VMEM((2, PE_E, PE_T), BF16), pltpu.VMEM((2, PE_E, PE_T), BF16),
                        pltpu.VMEM((D_MODEL, PE_T), F32)],
        compiler_params=_params(1, 61),
        name="peer",
    )(dn, up_t, xn3, cnt, e1, r2, e2, h2, g_final)


def _pack_w_in(w):
    o = 0
    segs = {}
    for name, sz in (("mq", M_QK_W // 2), ("mk", M_QK_W // 2), ("mv", M_V_W), ("mo", M_V_W),
                     ("mi", M_HEADS), ("mf", M_HEADS), ("aq", A_Q_W), ("ak", A_KV_HEADS * A_DH),
                     ("av", A_KV_HEADS * A_DH)):
        segs[name] = w[:, o:o + sz]
        o += sz
    k0, k1 = segs["ak"][:, :A_DH], segs["ak"][:, A_DH:]
    v0, v1 = segs["av"][:, :A_DH], segs["av"][:, A_DH:]
    gates = jnp.concatenate([segs["mi"], segs["mf"], jnp.zeros((w.shape[0], LANES - 2 * M_HEADS), w.dtype)], axis=1)
    packed = jnp.concatenate([segs["mq"], segs["mk"], segs["aq"] * (A_DH ** -0.5), k0, k0, k1, k1, gates], axis=1)
    w_t = jnp.concatenate([segs["mv"], segs["mo"], v0, v0, v1, v1], axis=1).T
    return packed.astype(BF16), w_t.astype(BF16)


def kernel(x, mem, g_mix, w_in, conv_w, conv_b, b_igate, b_fgate, g_mhead, sinks, w_out, g_cross, g_mem,
           w_xq, w_xk, w_xv, w_xo, g_ffn, w_pq, sub_keys1, sub_keys2, expert_down, expert_up, g_final):
    assert x.shape == (BATCH, SEQ, D_MODEL) and w_in.shape[0] == 1
    l = 0
    row = lambda v: v.reshape(1, -1).astype(F32)
    x2 = x.reshape(NTOK, D_MODEL)

    qk_pre, aq, ak, gates, mv_t, mo_t, av_t = _mix_in(x2, row(g_mix[l]), *_pack_w_in(w_in[l]))

    gate_bias = jnp.concatenate([b_igate[l], b_fgate[l], jnp.zeros((LANES - 2 * M_HEADS,), F32)]).reshape(1, LANES)
    g_mhead_b = jnp.broadcast_to(g_mhead[l].astype(F32)[:, None], (M_V_W, LANES))
    r3 = lambda a: a.reshape(BATCH, SEQ, a.shape[-1])
    hm = _mlstm(r3(qk_pre), mv_t, mo_t, r3(gates), conv_w[l], row(conv_b[l]), gate_bias, g_mhead_b)

    sinks_b = jnp.broadcast_to(sinks[l].astype(F32)[:, None], (A_HEADS, LANES))
    ha = _swa(r3(aq), r3(ak), av_t, sinks_b)

    kmem, vmem = _memkv(mem, row(g_mem[l]), w_xk[l].astype(BF16), w_xv[l].astype(BF16))

    h2, xn3 = _post(x2, hm.reshape(NTOK, M_V_W), ha.reshape(NTOK, A_Q_W), w_out[l].astype(BF16), row(g_cross[l]),
                    (w_xq[l] * (X_DH ** -0.5)).astype(BF16), kmem, vmem, w_xo[l].astype(BF16), row(g_ffn[l]))

    cnt, e1, r2, e2 = _route(xn3, w_pq[l].T.astype(BF16), sub_keys1[l], sub_keys2[l])

    out = _peer(expert_down[l], expert_up[l].T.astype(BF16), xn3, cnt, e1, r2, e2, h2, row(g_final))
    return out.reshape(BATCH, SEQ, D_MODEL)
```
